```python
import math
import jax, jax.numpy as jnp
from jax import lax
import numpy as np

D_MODEL = 1024
BATCH = 8
SEQ = 4096
DEPTH = 2

N_META = 16
N_BRANCH = 4
BRANCH_W = 512
CHUNK = 32
EPS = 1e-6
GLA_HEADS = 4
GLA_DK = 64
GLA_DV = 128
GLA_RANK = 16
GLA_TAU = 16.0
HG_HEADS = 4
HG_DK = 128
HG_DV = 128
S5_GROUP = 16
S5_GROUPS = BRANCH_W // S5_GROUP
S5_STATE = 64
LRU_BLOCKS = 8
LRU_BW = BRANCH_W // LRU_BLOCKS
CONV_W = 4
LRU_C = 8.0

IN_SIZES = (
    GLA_HEADS * GLA_DK, GLA_HEADS * GLA_DK, GLA_HEADS * GLA_DV, GLA_RANK, BRANCH_W,
    HG_HEADS * HG_DK, HG_HEADS * HG_DK, HG_HEADS * HG_DV, BRANCH_W,
    BRANCH_W, BRANCH_W,
    BRANCH_W, BRANCH_W,
    N_BRANCH * D_MODEL,
)
IN_WIDTH = sum(IN_SIZES)

kernel_name = 'hybrid_gla_hgrn2_s5_rglru_meta'


def _rms_norm(x, gain):
    x32 = x.astype(jnp.float32)
    y = x32 * lax.rsqrt(jnp.mean(x32 * x32, axis=-1, keepdims=True) + EPS)
    return (y * gain.astype(jnp.float32)).astype(x.dtype)


def _to_heads(t, heads):
    b, s, _ = t.shape
    return t.reshape(b, s, heads, -1).transpose(0, 2, 1, 3)


def _from_heads(t):
    b, h, s, d = t.shape
    return t.transpose(0, 2, 1, 3).reshape(b, s, h * d)


def _chunked_gated_linear_attention(q, k, v, log_f):
    f32 = jnp.float32
    q, k, v, log_f = (t.astype(f32) for t in (q, k, v, log_f))
    t_len = q.shape[2]
    pad = (-t_len) % CHUNK
    padw = ((0, 0), (0, 0), (pad, 0), (0, 0))
    q, k, v, log_f = (jnp.pad(t, padw) for t in (q, k, v, log_f))
    bsz, heads, t_pad, dk = q.shape
    dv = v.shape[-1]
    n = t_pad // CHUNK
    q, k, v, log_f = (t.reshape(bsz, heads, n, CHUNK, t.shape[-1]) for t in (q, k, v, log_f))
    g_cum = jnp.cumsum(log_f, axis=3)
    g_last = g_cum[:, :, :, -1:, :]
    q_dec = q * jnp.exp(g_cum)
    k_inv = k * jnp.exp(-g_cum)
    k_end = k * jnp.exp(g_last - g_cum)
    causal = jnp.tril(jnp.ones((CHUNK, CHUNK), dtype=bool))
    scores = jnp.where(causal, jnp.einsum('bhncd,bhnsd->bhncs', q_dec, k_inv), 0.0)
    o_intra = jnp.einsum('bhncs,bhnsv->bhncv', scores, v)
    kv = jnp.einsum('bhncd,bhncv->nbhdv', k_end, v)
    decay = jnp.exp(g_last[:, :, :, 0, :]).transpose(2, 0, 1, 3)

    def step(state, inp):
        d, upd = inp
        return d[..., None] * state + upd, state

    _, s_start = lax.scan(step, jnp.zeros((bsz, heads, dk, dv), f32), (decay, kv))
    o_inter = jnp.einsum('bhncd,nbhdv->bhncv', q_dec, s_start)
    o = (o_intra + o_inter).reshape(bsz, heads, t_pad, dv)
    return o[:, :, pad:]


def _gla_branch(q, k, v, lr, gate, w_lr, b_lr, norm_g):
    log_a = jax.nn.log_sigmoid((lr @ w_lr + b_lr).astype(jnp.float32)) / GLA_TAU
    o = _chunked_gated_linear_attention(
        _to_heads(q, GLA_HEADS) * (GLA_DK ** -0.5), _to_heads(k, GLA_HEADS),
        _to_heads(v, GLA_HEADS), _to_heads(log_a, GLA_HEADS))
    o = _rms_norm(o, norm_g[:, None, :])
    return _from_heads(o).astype(gate.dtype) * jax.nn.silu(gate)


def _hgrn2_branch(q, f_logit, i_in, gate, lb, norm_g):
    f = lb + (1.0 - lb) * jax.nn.sigmoid(f_logit.astype(jnp.float32))
    log_f = jnp.log(f)
    k = 1.0 - f
    o = _chunked_gated_linear_attention(
        _to_heads(q, HG_HEADS), _to_heads(k, HG_HEADS),
        _to_heads(i_in, HG_HEADS), _to_heads(log_f, HG_HEADS))
    o = _rms_norm(o, norm_g[:, None, :])
    return _from_heads(o).astype(gate.dtype) * jax.nn.silu(gate)


def _complex_linear_combine(e1, e2):
    a1r, a1i, b1r, b1i = e1
    a2r, a2i, b2r, b2i = e2
    return (a2r * a1r - a2i * a1i,
            a2r * a1i + a2i * a1r,
            a2r * b1r - a2i * b1i + b2r,
            a2r * b1i + a2i * b1r + b2i)


def _s5_branch(u, gate, lam_re, lam_im, log_dt, b_re, b_im, c_re, c_im, d_skip, glu_w, glu_b):
    f32 = jnp.float32
    bsz, t_len, _ = u.shape
    u32 = u.astype(f32).reshape(bsz, t_len, S5_GROUPS, S5_GROUP)
    lr, li = lam_re.astype(f32), lam_im.astype(f32)
    dt = jnp.exp(log_dt.astype(f32))[:, None]
    mag = jnp.exp(lr * dt)
    ab_re, ab_im = mag * jnp.cos(li * dt), mag * jnp.sin(li * dt)
    nr, ni = ab_re - 1.0, ab_im
    den = lr * lr + li * li
    cr, ci = (nr * lr + ni * li) / den, (ni * lr - nr * li) / den
    br, bi = b_re.astype(f32), b_im.astype(f32)
    bb_re = cr[..., None] * br - ci[..., None] * bi
    bb_im = cr[..., None] * bi + ci[..., None] * br
    bu_re = jnp.einsum('btgc,gnc->tbgn', u32, bb_re)
    bu_im = jnp.einsum('btgc,gnc->tbgn', u32, bb_im)
    a_shape = (t_len, 1, S5_GROUPS, S5_STATE)
    a_re = jnp.broadcast_to(ab_re[None, None], a_shape)
    a_im = jnp.broadcast_to(ab_im[None, None], a_shape)
    _, _, h_re, h_im = lax.associative_scan(_complex_linear_combine, (a_re, a_im, bu_re, bu_im), axis=0)
    y = (jnp.einsum('tbgn,gcn->btgc', h_re, c_re.astype(f32))
         - jnp.einsum('tbgn,gcn->btgc', h_im, c_im.astype(f32)))
    y = y.reshape(bsz, t_len, BRANCH_W) + d_skip.astype(f32) * u32.reshape(bsz, t_len, BRANCH_W)
    y = jax.nn.gelu(y)
    y = y * jax.nn.sigmoid(y @ glu_w.astype(f32) + glu_b.astype(f32))
    return y.astype(gate.dtype) * jax.nn.silu(gate)


def _real_linear_combine(e1, e2):
    a1, b1 = e1
    a2, b2 = e2
    return a1 * a2, a2 * b1 + b2


def _rglru_branch(xb, gate, conv_w, conv_b, wa, ba, wx, bx, lam):
    f32 = jnp.float32
    bsz, t_len, width = xb.shape
    xp = jnp.pad(xb, ((0, 0), (CONV_W - 1, 0), (0, 0)))
    xc = conv_b + sum(xp[:, CONV_W - 1 - j: CONV_W - 1 - j + t_len, :] * conv_w[j] for j in range(CONV_W))
    xc32 = xc.astype(f32)
    blocks = xc32.reshape(bsz, t_len, LRU_BLOCKS, LRU_BW)
    r = jax.nn.sigmoid(jnp.einsum('btki,kij->btkj', blocks, wa.astype(f32)).reshape(bsz, t_len, width) + ba)
    i_g = jax.nn.sigmoid(jnp.einsum('btki,kij->btkj', blocks, wx.astype(f32)).reshape(bsz, t_len, width) + bx)
    log_a = -LRU_C * r * jax.nn.softplus(-lam.astype(f32))
    a = jnp.exp(log_a)
    inp = jnp.sqrt(-jnp.expm1(2.0 * log_a)) * (i_g * xc32)
    _, h = lax.associative_scan(_real_linear_combine, (a.transpose(1, 0, 2), inp.transpose(1, 0, 2)), axis=0)
    return h.transpose(1, 0, 2).astype(gate.dtype) * jax.nn.silu(gate)


def _hybrid_layer(h, norm_g, w_in, w_branch, w_out, gla_w_lr, gla_b_lr, gla_norm, hg_norm, hg_lb,
                  s5_lambda_re, s5_lambda_im, s5_log_dt, s5_b_re, s5_b_im, s5_c_re, s5_c_im,
                  s5_d, s5_glu_w, s5_glu_b, lru_conv_w, lru_conv_b, lru_wa, lru_ba, lru_wx, lru_bx,
                  lru_lambda):
    bsz, t_len, _ = h.shape
    z = _rms_norm(h, norm_g) @ w_in
    offsets = np.cumsum(IN_SIZES)[:-1].tolist()
    (gq, gk, gv, glr, ggate, hq, hf, hi, hgate, su, sgate, lx, lgate, mg) = jnp.split(z, offsets, axis=-1)
    y_a = _gla_branch(gq, gk, gv, glr, ggate, gla_w_lr, gla_b_lr, gla_norm)
    y_b = _hgrn2_branch(hq, hf, hi, hgate, hg_lb, hg_norm)
    y_c = _s5_branch(su, sgate, s5_lambda_re, s5_lambda_im, s5_log_dt, s5_b_re, s5_b_im,
                     s5_c_re, s5_c_im, s5_d, s5_glu_w, s5_glu_b)
    y_d = _rglru_branch(lx, lgate, lru_conv_w, lru_conv_b, lru_wa, lru_ba, lru_wx, lru_bx, lru_lambda)
    ys = jnp.stack([y_a, y_b, y_c, y_d], axis=2)
    proj = jnp.einsum('btnw,nwd->btnd', ys, w_branch)
    gates = jax.nn.sigmoid(mg.reshape(bsz, t_len, N_BRANCH, D_MODEL))
    merged = jnp.sum(gates * proj, axis=2)
    return h + merged @ w_out


def setup_inputs(seed: int = 0) -> dict:
    key = jax.random.key(seed)
    ks = iter(jax.random.split(key, 40))
    f32 = jnp.float32

    def nrm(shape, scale):
        return scale * jax.random.normal(next(ks), shape, f32)

    W, G, N = BRANCH_W, S5_GROUPS, S5_STATE
    x = nrm((BATCH, SEQ, D_MODEL), 1.0)
    meta_tokens = nrm((N_META, D_MODEL), 1.0)
    hgrn_lb_logits = nrm((DEPTH, HG_HEADS * HG_DK), 0.1)
    final_norm = 1.0 + nrm((D_MODEL,), 0.02)
    norm_g = 1.0 + nrm((DEPTH, D_MODEL), 0.02)
    w_in = nrm((DEPTH, D_MODEL, IN_WIDTH), D_MODEL ** -0.5)
    w_branch = nrm((DEPTH, N_BRANCH, W, D_MODEL), W ** -0.5)
    w_out = nrm((DEPTH, D_MODEL, D_MODEL), D_MODEL ** -0.5)
    gla_w_lr = nrm((DEPTH, GLA_RANK, GLA_HEADS * GLA_DK), GLA_RANK ** -0.5)
    gla_b_lr = nrm((DEPTH, GLA_HEADS * GLA_DK), 0.1)
    gla_norm = 1.0 + nrm((DEPTH, GLA_HEADS, GLA_DV), 0.02)
    hg_norm = 1.0 + nrm((DEPTH, HG_HEADS, HG_DV), 0.02)
    s5_lambda_re = -0.5 + nrm((DEPTH, G, N), 0.01)
    s5_lambda_im = math.pi * jnp.arange(N, dtype=f32)[None, None, :] + nrm((DEPTH, G, N), 0.01)
    s5_log_dt = jax.random.uniform(next(ks), (DEPTH, G), f32, math.log(1e-3), math.log(1e-1))
    s5_b_re = nrm((DEPTH, G, N, S5_GROUP), (2.0 * S5_GROUP) ** -0.5)
    s5_b_im = nrm((DEPTH, G, N, S5_GROUP), (2.0 * S5_GROUP) ** -0.5)
    s5_c_re = nrm((DEPTH, G, S5_GROUP, N), (2.0 * N) ** -0.5 * 4.0)
    s5_c_im = nrm((DEPTH, G, S5_GROUP, N), (2.0 * N) ** -0.5 * 4.0)
    s5_d = nrm((DEPTH, W), 1.0)
    s5_glu_w = nrm((DEPTH, W, W), W ** -0.5)
    s5_glu_b = nrm((DEPTH, W), 0.02)
    lru_conv_w = nrm((DEPTH, CONV_W, W), CONV_W ** -0.5)
    lru_conv_b = nrm((DEPTH, W), 0.02)
    lru_wa = nrm((DEPTH, LRU_BLOCKS, LRU_BW, LRU_BW), LRU_BW ** -0.5)
    lru_ba = nrm((DEPTH, W), 0.02)
    lru_wx = nrm((DEPTH, LRU_BLOCKS, LRU_BW, LRU_BW), LRU_BW ** -0.5)
    lru_bx = nrm((DEPTH, W), 0.02)
    a_c = jax.random.uniform(next(ks), (DEPTH, W), f32, 0.9, 0.999)
    a0 = a_c ** (1.0 / LRU_C)
    lru_lambda = jnp.log(a0) - jnp.log1p(-a0)
    return {
        'x': x, 'meta_tokens': meta_tokens, 'hgrn_lb_logits': hgrn_lb_logits, 'final_norm': final_norm,
        'norm_g': norm_g, 'w_in': w_in, 'w_branch': w_branch, 'w_out': w_out,
        'gla_w_lr': gla_w_lr, 'gla_b_lr': gla_b_lr, 'gla_norm': gla_norm, 'hg_norm': hg_norm,
        's5_lambda_re': s5_lambda_re, 's5_lambda_im': s5_lambda_im, 's5_log_dt': s5_log_dt,
        's5_b_re': s5_b_re, 's5_b_im': s5_b_im, 's5_c_re': s5_c_re, 's5_c_im': s5_c_im,
        's5_d': s5_d, 's5_glu_w': s5_glu_w, 's5_glu_b': s5_glu_b,
        'lru_conv_w': lru_conv_w, 'lru_conv_b': lru_conv_b, 'lru_wa': lru_wa, 'lru_ba': lru_ba,
        'lru_wx': lru_wx, 'lru_bx': lru_bx, 'lru_lambda': lru_lambda,
    }


def reference(x, meta_tokens, hgrn_lb_logits, final_norm, norm_g, w_in, w_branch, w_out,
              gla_w_lr, gla_b_lr, gla_norm, hg_norm, s5_lambda_re, s5_lambda_im, s5_log_dt,
              s5_b_re, s5_b_im, s5_c_re, s5_c_im, s5_d, s5_glu_w, s5_glu_b,
              lru_conv_w, lru_conv_b, lru_wa, lru_ba, lru_wx, lru_bx, lru_lambda):
    bsz = x.shape[0]
    meta = jnp.broadcast_to(meta_tokens.astype(x.dtype)[None], (bsz, N_META, D_MODEL))
    h = jnp.concatenate([meta, x], axis=1)
    p = jax.nn.softmax(hgrn_lb_logits.astype(jnp.float32), axis=0)
    lb_all = jnp.cumsum(p, axis=0) - p[0:1]
    for l in range(DEPTH):
        h = _hybrid_layer(h, norm_g[l], w_in[l], w_branch[l], w_out[l], gla_w_lr[l], gla_b_lr[l],
                          gla_norm[l], hg_norm[l], lb_all[l],
                          s5_lambda_re[l], s5_lambda_im[l], s5_log_dt[l], s5_b_re[l], s5_b_im[l],
                          s5_c_re[l], s5_c_im[l], s5_d[l], s5_glu_w[l], s5_glu_b[l],
                          lru_conv_w[l], lru_conv_b[l], lru_wa[l], lru_ba[l], lru_wx[l], lru_bx[l],
                          lru_lambda[l])
    return _rms_norm(h, final_norm)[:, N_META:]
```

```python
import functools
import math

import jax
import jax.numpy as jnp
from jax import lax
from jax.experimental import pallas as pl
from jax.experimental.pallas import tpu as pltpu

F32 = jnp.float32
BF16 = jnp.bfloat16

N_META = 16
EPS = 1e-6
STEPS = 32
LANES = 128
HEAD_DV = 128
BRANCH_W = 512
GLA_HEADS, GLA_DK, GLA_TAU = 4, 64, 16.0
HG_HEADS, HG_DK = 4, 128
S5_GROUP, S5_STATE = 16, 64
S5_BLOCKS = 4
S5_BLOCK_STATE = 2 * 8 * S5_STATE
LRU_BLOCKS, LRU_C, CONV_W = 8, 8.0, 4
VMEM_LIMIT_BYTES = 58 * 1024 * 1024


def _sigmoid(x):
    return 1.0 / (1.0 + jnp.exp(-x))


def _softplus(x):
    return jnp.maximum(x, 0.0) + jnp.log1p(jnp.exp(-jnp.abs(x)))


def _silu(x):
    return x * _sigmoid(x)


def _gelu_tanh(x):
    c = math.sqrt(2.0 / math.pi)
    return 0.5 * x * (1.0 + jnp.tanh(c * (x + 0.044715 * (x * x * x))))


def _rms_norm(x, gain):
    return x * lax.rsqrt(jnp.mean(x * x, axis=-1, keepdims=True) + EPS) * gain


def _mm(a, w):
    return jnp.dot(a.astype(BF16), w, preferred_element_type=F32)


def _mm_nt(a, b):
    return lax.dot_general(a.astype(BF16), b.astype(BF16), (((1,), (1,)), ((), ())),
                           preferred_element_type=F32)


def _mm_tn(a, b):
    return lax.dot_general(a.astype(BF16), b.astype(BF16), (((0,), (0,)), ((), ())),
                           preferred_element_type=F32)


def _head_lanes(arr, h, dk):
    p = (h * dk) // LANES
    blk = arr[:, p * LANES:(p + 1) * LANES]
    if dk == LANES:
        return blk, blk
    assert dk * 2 == LANES
    lane = lax.broadcasted_iota(jnp.int32, blk.shape, 1)
    in_head = (lane < dk) if (h % 2 == 0) else (lane >= dk)
    swapped = pltpu.roll(blk, dk, axis=1)
    return jnp.where(in_head, blk, swapped), jnp.where(in_head, blk, 0.0)


def _chunk_attention(q, k, v, g, st_ref, norm_g, dk, heads, nb):
    rows, width = q.shape
    steps = rows // nb
    g3 = g.reshape(steps, nb, width)
    acc = g3[0]
    cum = [acc]
    for t in range(1, steps):
        acc = acc + g3[t]
        cum.append(acc)
    g_cum = jnp.concatenate(cum, axis=0)
    g_last = acc
    g_last_rows = jnp.concatenate([g_last] * steps, axis=0)
    q_dec = q * jnp.exp(g_cum)
    k_inv = k * jnp.exp(-g_cum)
    k_end = k * jnp.exp(g_last_rows - g_cum)

    exp_w = nb * dk
    reps = exp_w // LANES
    r_i = lax.broadcasted_iota(jnp.int32, (rows, rows), 0)
    c_i = lax.broadcasted_iota(jnp.int32, (rows, rows), 1)
    causal_same_batch = ((r_i % nb) == (c_i % nb)) & ((c_i // nb) <= (r_i // nb))
    row_b = lax.broadcasted_iota(jnp.int32, (rows, exp_w), 0) % nb
    col_b = lax.broadcasted_iota(jnp.int32, (rows, exp_w), 1) // dk
    batch_mask = row_b == col_b
    batch_mask_nb = (lax.broadcasted_iota(jnp.int32, (nb, exp_w), 0)
                     == lax.broadcasted_iota(jnp.int32, (nb, exp_w), 1) // dk)

    outs = []
    for h in range(heads):
        v_h = v[:, h * HEAD_DV:(h + 1) * HEAD_DV].astype(BF16)
        q_dup, q_own = _head_lanes(q_dec, h, dk)
        _, kinv_own = _head_lanes(k_inv, h, dk)
        kend_dup, _ = _head_lanes(k_end, h, dk)
        gl_dup, _ = _head_lanes(g_last, h, dk)

        scores = jnp.where(causal_same_batch, _mm_nt(q_own, kinv_own), 0.0)
        o_intra = _mm(scores, v_h)

        q_exp = jnp.where(batch_mask, jnp.concatenate([q_dup] * reps, axis=1), 0.0)
        k_exp = jnp.where(batch_mask, jnp.concatenate([kend_dup] * reps, axis=1), 0.0)
        st = st_ref[h]
        o = o_intra + _mm_nt(q_exp, st)

        gl_flat = jnp.sum(jnp.where(batch_mask_nb, jnp.concatenate([gl_dup] * reps, axis=1), 0.0),
                          axis=0, keepdims=True)
        st_ref[h] = st * jnp.exp(gl_flat) + _mm_tn(v_h, k_exp)

        outs.append(_rms_norm(o, norm_g[h:h + 1, :]))
    return jnp.concatenate(outs, axis=1)


def _layer_kernel(h_ref, norm_g_ref,
                  w_gla_qkv_ref, w_gla_lr_ref, w_gla_gate_ref, gla_w_lr_ref, gla_b_lr_ref, gla_norm_ref,
                  w_hg_ref, hg_lb_ref, hg_norm_ref,
                  w_s5_ref, s5_bw_ref, s5_cw_ref, s5_are_ref, s5_aim_ref, s5_d_ref, s5_glu_w_ref,
                  s5_glu_b_ref,
                  w_lru_ref, lru_conv_w_ref, lru_conv_b_ref, lru_wax_ref, lru_bax_ref, lru_lam_ref,
                  w_mg_ref, w_branch_ref, w_out_ref, final_norm_ref,
                  out_ref,
                  gla_st_ref, hg_st_ref, s5_scan_ref, s5_st_ref, lru_st_ref, conv_st_ref,
                  *, nb, final):
    @pl.when(pl.program_id(0) == 0)
    def _init():
        gla_st_ref[...] = jnp.zeros_like(gla_st_ref)
        hg_st_ref[...] = jnp.zeros_like(hg_st_ref)
        s5_st_ref[...] = jnp.zeros_like(s5_st_ref)
        lru_st_ref[...] = jnp.zeros_like(lru_st_ref)
        conv_st_ref[...] = jnp.zeros_like(conv_st_ref)

    h = h_ref[...]
    rows = h.shape[0]
    steps = rows // nb
    xn = _rms_norm(h, norm_g_ref[...]).astype(BF16)

    qkv = _mm(xn, w_gla_qkv_ref[...])
    w_q = GLA_HEADS * GLA_DK
    lr = _mm(xn, w_gla_lr_ref[...])
    logit = _mm(lr, gla_w_lr_ref[...]) + gla_b_lr_ref[...]
    log_a = -_softplus(-logit) * (1.0 / GLA_TAU)
    y_a = _chunk_attention(qkv[:, :w_q] * (GLA_DK ** -0.5), qkv[:, w_q:2 * w_q], qkv[:, 2 * w_q:],
                           log_a, gla_st_ref, gla_norm_ref[...], GLA_DK, GLA_HEADS, nb)
    y_a = y_a * _silu(_mm(xn, w_gla_gate_ref[...]))

    hz = _mm(xn, w_hg_ref[...])
    w_h = HG_HEADS * HG_DK
    lb = hg_lb_ref[...]
    f = lb + (1.0 - lb) * _sigmoid(hz[:, w_h:2 * w_h])
    y_b = _chunk_attention(hz[:, :w_h], 1.0 - f, hz[:, 2 * w_h:3 * w_h], jnp.log(f),
                           hg_st_ref, hg_norm_ref[...], HG_DK, HG_HEADS, nb)
    y_b = y_b * _silu(hz[:, 3 * w_h:])

    sz = _mm(xn, w_s5_ref[...])
    u = sz[:, :BRANCH_W]
    half = S5_BLOCK_STATE // 2
    y_blocks = []
    for j in range(S5_BLOCKS):
        s5_scan_ref[...] = _mm(u[:, j * LANES:(j + 1) * LANES], s5_bw_ref[j])
        a_re = jnp.broadcast_to(s5_are_ref[j:j + 1, :], (nb, half))
        a_im = jnp.broadcast_to(s5_aim_ref[j:j + 1, :], (nb, half))
        s_re = s5_st_ref[j, :, :half]
        s_im = s5_st_ref[j, :, half:]
        for t in range(steps):
            b_re = s5_scan_ref[t * nb:(t + 1) * nb, :half]
            b_im = s5_scan_ref[t * nb:(t + 1) * nb, half:]
            s_re, s_im = (a_re * s_re - a_im * s_im + b_re, a_re * s_im + a_im * s_re + b_im)
            s5_scan_ref[t * nb:(t + 1) * nb, :half] = s_re
            s5_scan_ref[t * nb:(t + 1) * nb, half:] = s_im
        s5_st_ref[j, :, :half] = s_re
        s5_st_ref[j, :, half:] = s_im
        y_blocks.append(_mm(s5_scan_ref[...], s5_cw_ref[j]))
    y_c = jnp.concatenate(y_blocks, axis=1) + s5_d_ref[...] * u
    y_c = _gelu_tanh(y_c)
    y_c = y_c * _sigmoid(_mm(y_c, s5_glu_w_ref[...]) + s5_glu_b_ref[...])
    y_c = y_c * _silu(sz[:, BRANCH_W:])

    lz = _mm(xn, w_lru_ref[...])
    lx = lz[:, :BRANCH_W]
    hist = (CONV_W - 1) * nb
    x_full = jnp.concatenate([conv_st_ref[...], lx], axis=0)
    conv_st_ref[...] = lx[rows - hist:, :]
    xc = lru_conv_b_ref[...]
    for j in range(CONV_W):
        xc = xc + x_full[hist - j * nb: hist - j * nb + rows, :] * lru_conv_w_ref[j:j + 1, :]
    ax = _mm(xc, lru_wax_ref[...]) + lru_bax_ref[...]
    r_g = _sigmoid(ax[:, :BRANCH_W])
    i_g = _sigmoid(ax[:, BRANCH_W:])
    log_al = (-LRU_C) * r_g * _softplus(-lru_lam_ref[...])
    a_l = jnp.exp(log_al)
    inp = jnp.sqrt(1.0 - jnp.exp(2.0 * log_al)) * (i_g * xc)
    s_l = lru_st_ref[...]
    hs = []
    for t in range(steps):
        s_l = a_l[t * nb:(t + 1) * nb, :] * s_l + inp[t * nb:(t + 1) * nb, :]
        hs.append(s_l)
    lru_st_ref[...] = s_l
    y_d = jnp.concatenate(hs, axis=0) * _silu(lz[:, BRANCH_W:])

    d_model = h.shape[1]
    merged = None
    for n, y_n in enumerate((y_a, y_b, y_c, y_d)):
        gate = _sigmoid(_mm(xn, w_mg_ref[:, n * d_model:(n + 1) * d_model]))
        term = gate * _mm(y_n, w_branch_ref[n])
        merged = term if merged is None else merged + term
    out = h + _mm(merged, w_out_ref[...])
    if final:
        out = _rms_norm(out, final_norm_ref[...])
    out_ref[...] = out


def _resident(arr):
    nd = arr.ndim
    return pl.BlockSpec(arr.shape, lambda i, _nd=nd: (0,) * _nd, pipeline_mode=pl.Buffered(1))


def _layer_call(h, params, final_norm, *, nb, final):
    n_rows, d_model = h.shape
    rows = STEPS * nb
    weights = list(params) + [final_norm]
    kern = functools.partial(_layer_kernel, nb=nb, final=final)
    return pl.pallas_call(
        kern,
        grid=(n_rows // rows,),
        in_specs=[pl.BlockSpec((rows, d_model), lambda i: (i, 0))] + [_resident(w) for w in weights],
        out_specs=pl.BlockSpec((rows, d_model), lambda i: (i, 0)),
        out_shape=jax.ShapeDtypeStruct((n_rows, d_model), F32),
        scratch_shapes=[
            pltpu.VMEM((GLA_HEADS, HEAD_DV, nb * GLA_DK), F32),
            pltpu.VMEM((HG_HEADS, HEAD_DV, nb * HG_DK), F32),
            pltpu.VMEM((rows, S5_BLOCK_STATE), F32),
            pltpu.VMEM((S5_BLOCKS, nb, S5_BLOCK_STATE), F32),
            pltpu.VMEM((nb, BRANCH_W), F32),
            pltpu.VMEM(((CONV_W - 1) * nb, BRANCH_W), F32),
        ],
        compiler_params=pltpu.CompilerParams(dimension_semantics=("arbitrary",),
                                             vmem_limit_bytes=VMEM_LIMIT_BYTES),
        name="hybrid_layer_final" if final else "hybrid_layer",
    )(h, *weights)


def _block_diag(blocks):
    n, r, c = blocks.shape
    eye = jnp.eye(n, dtype=blocks.dtype)
    return (blocks[:, :, None, :] * eye[:, None, :, None]).reshape(n * r, n * c)


def _layer_params(l, lb, norm_g, w_in, w_branch, w_out, gla_w_lr, gla_b_lr, gla_norm, hg_norm,
                  s5_lambda_re, s5_lambda_im, s5_log_dt, s5_b_re, s5_b_im, s5_c_re, s5_c_im, s5_d,
                  s5_glu_w, s5_glu_b, lru_conv_w, lru_conv_b, lru_wa, lru_ba, lru_wx, lru_bx, lru_lambda):
    w = w_in[l]
    d_model = w.shape[0]
    sizes = (GLA_HEADS * GLA_DK, GLA_HEADS * GLA_DK, GLA_HEADS * HEAD_DV, 16, BRANCH_W,
             HG_HEADS * HG_DK, HG_HEADS * HG_DK, HG_HEADS * HEAD_DV, BRANCH_W,
             BRANCH_W, BRANCH_W, BRANCH_W, BRANCH_W, 4 * d_model)
    offs = [0]
    for s in sizes:
        offs.append(offs[-1] + s)
    col = lambda a, b: w[:, offs[a]:offs[b]].astype(BF16)
    rank = sizes[3]
    w_gla_lr = jnp.pad(w[:, offs[3]:offs[4]], ((0, 0), (0, LANES - rank))).astype(BF16)
    gla_w_lr_p = jnp.pad(gla_w_lr[l], ((0, LANES - rank), (0, 0))).astype(BF16)

    lr_, li_ = s5_lambda_re[l].astype(F32), s5_lambda_im[l].astype(F32)
    dt = jnp.exp(s5_log_dt[l].astype(F32))[:, None]
    mag = jnp.exp(lr_ * dt)
    ab_re, ab_im = mag * jnp.cos(li_ * dt), mag * jnp.sin(li_ * dt)
    nr, ni = ab_re - 1.0, ab_im
    den = lr_ * lr_ + li_ * li_
    cr, ci = (nr * lr_ + ni * li_) / den, (ni * lr_ - nr * li_) / den
    br, bi = s5_b_re[l].astype(F32), s5_b_im[l].astype(F32)
    bb_re = cr[..., None] * br - ci[..., None] * bi
    bb_im = cr[..., None] * bi + ci[..., None] * br
    gpb = LANES // S5_GROUP

    def in_block(bb):
        t = jnp.swapaxes(bb, 1, 2).reshape(S5_BLOCKS, gpb, S5_GROUP, S5_STATE)
        return jnp.stack([_block_diag(t[j]) for j in range(S5_BLOCKS)])

    def out_block(c):
        t = jnp.swapaxes(c.astype(F32), 1, 2).reshape(S5_BLOCKS, gpb, S5_STATE, S5_GROUP)
        return jnp.stack([_block_diag(t[j]) for j in range(S5_BLOCKS)])

    s5_bw = jnp.concatenate([in_block(bb_re), in_block(bb_im)], axis=2).astype(BF16)
    s5_cw = jnp.concatenate([out_block(s5_c_re[l]), -out_block(s5_c_im[l])], axis=1).astype(BF16)
    s5_are = ab_re.reshape(S5_BLOCKS, gpb * S5_STATE)
    s5_aim = ab_im.reshape(S5_BLOCKS, gpb * S5_STATE)

    lru_wax = jnp.concatenate([_block_diag(lru_wa[l]), _block_diag(lru_wx[l])], axis=1).astype(BF16)
    lru_bax = jnp.concatenate([lru_ba[l], lru_bx[l]])[None, :]
    row = lambda a: a[None, :].astype(F32)
    return [
        row(norm_g[l]),
        jnp.concatenate([col(0, 1), col(1, 2), col(2, 3)], axis=1), w_gla_lr, col(4, 5),
        gla_w_lr_p, row(gla_b_lr[l]), gla_norm[l].astype(F32),
        jnp.concatenate([col(5, 6), col(6, 7), col(7, 8), col(8, 9)], axis=1), row(lb), hg_norm[l].astype(F32),
        jnp.concatenate([col(9, 10), col(10, 11)], axis=1), s5_bw, s5_cw, s5_are, s5_aim, row(s5_d[l]),
        s5_glu_w[l].astype(BF16), row(s5_glu_b[l]),
        jnp.concatenate([col(11, 12), col(12, 13)], axis=1), lru_conv_w[l].astype(F32), row(lru_conv_b[l]),
        lru_wax, lru_bax.astype(F32), row(lru_lambda[l]),
        col(13, 14), w_branch[l].astype(BF16), w_out[l].astype(BF16),
    ]


def kernel(x, meta_tokens, hgrn_lb_logits, final_norm, norm_g, w_in, w_branch, w_out, gla_w_lr, gla_b_lr, gla_norm, hg_norm, s5_lambda_re, s5_lambda_im, s5_log_dt, s5_b_re, s5_b_im, s5_c_re, s5_c_im, s5_d, s5_glu_w, s5_glu_b, lru_conv_w, lru_conv_b, lru_wa, lru_ba, lru_wx, lru_bx, lru_lambda):
    nb, seq, d_model = x.shape
    depth = w_in.shape[0]
    t_len = N_META + seq
    t_pad = -(-t_len // STEPS) * STEPS
    meta = jnp.broadcast_to(meta_tokens.astype(x.dtype)[:, None, :], (N_META, nb, d_model))
    h = jnp.concatenate([meta, jnp.swapaxes(x, 0, 1),
                         jnp.zeros((t_pad - t_len, nb, d_model), x.dtype)], axis=0)
    h = h.reshape(t_pad * nb, d_model)

    p = jax.nn.softmax(hgrn_lb_logits.astype(F32), axis=0)
    lb_all = jnp.cumsum(p, axis=0) - p[0:1]
    fin = final_norm[None, :].astype(F32)
    for l in range(depth):
        params = _layer_params(l, lb_all[l], norm_g, w_in, w_branch, w_out, gla_w_lr, gla_b_lr, gla_norm,
                               hg_norm, s5_lambda_re, s5_lambda_im, s5_log_dt, s5_b_re, s5_b_im, s5_c_re,
                               s5_c_im, s5_d, s5_glu_w, s5_glu_b, lru_conv_w, lru_conv_b, lru_wa, lru_ba,
                               lru_wx, lru_bx, lru_lambda)
        h = _layer_call(h, params, fin, nb=nb, final=(l == depth - 1))
    out = h.reshape(t_pad, nb, d_model)[N_META:t_len]
    return jnp.swapaxes(out, 0, 1)
```

```python
import functools
import math

import jax
import jax.numpy as jnp
from jax import lax
from jax.experimental import pallas as pl
from jax.experimental.pallas import tpu as pltpu

F32 = jnp.float32
BF16 = jnp.bfloat16

N_META = 16
EPS = 1e-6
STEPS = 32
LANES = 128
HEAD_DV = 128
BRANCH_W = 512
GLA_HEADS, GLA_DK, GLA_TAU = 4, 64, 16.0
HG_HEADS, HG_DK = 4, 128
S5_GROUP, S5_STATE = 16, 64
S5_BLOCKS = 4
S5_BLOCK_STATE = 2 * 8 * S5_STATE
N_BRANCH = 4
LRU_BLOCKS, LRU_C, CONV_W = 8, 8.0, 4
VMEM_LIMIT_BYTES = 58 * 1024 * 1024


def _sigmoid(x):
    return 1.0 / (1.0 + jnp.exp(-x))


def _softplus(x):
    return jnp.maximum(x, 0.0) + jnp.log1p(jnp.exp(-jnp.abs(x)))


def _silu(x):
    return x * _sigmoid(x)


def _gelu_tanh(x):
    c = math.sqrt(2.0 / math.pi)
    return 0.5 * x * (1.0 + jnp.tanh(c * (x + 0.044715 * (x * x * x))))


def _rms_norm(x, gain):
    return x * lax.rsqrt(jnp.mean(x * x, axis=-1, keepdims=True) + EPS) * gain


def _mm(a, w):
    return jnp.dot(a.astype(BF16), w, preferred_element_type=F32)


def _mm_nt(a, b):
    return lax.dot_general(a.astype(BF16), b.astype(BF16), (((1,), (1,)), ((), ())),
                           preferred_element_type=F32)


def _head_lanes(arr, h, dk):
    p = (h * dk) // LANES
    blk = arr[:, p * LANES:(p + 1) * LANES]
    if dk == LANES:
        return blk, blk
    assert dk * 2 == LANES
    lane = lax.broadcasted_iota(jnp.int32, blk.shape, 1)
    in_head = (lane < dk) if (h % 2 == 0) else (lane >= dk)
    swapped = pltpu.roll(blk, dk, axis=1)
    return jnp.where(in_head, blk, swapped), jnp.where(in_head, blk, 0.0)


def _chunk_attention(q, k, v, g, st_ref, norm_g, dk, heads, nb, between_heads):
    rows, width = q.shape
    steps = rows // nb
    g3 = g.reshape(steps, nb, width)
    acc = g3[0]
    cum = [acc]
    for t in range(1, steps):
        acc = acc + g3[t]
        cum.append(acc)
    g_cum = jnp.concatenate(cum, axis=0)
    g_last = acc
    g_last_rows = jnp.concatenate([g_last] * steps, axis=0)
    q_dec = q * jnp.exp(g_cum)
    k_inv = k * jnp.exp(-g_cum)
    k_end = k * jnp.exp(g_last_rows - g_cum)

    exp_w = nb * dk
    reps = exp_w // LANES
    r_i = lax.broadcasted_iota(jnp.int32, (rows, rows), 0)
    c_i = lax.broadcasted_iota(jnp.int32, (rows, rows), 1)
    causal_same_batch = ((r_i % nb) == (c_i % nb)) & ((c_i // nb) <= (r_i // nb))
    row_b = lax.broadcasted_iota(jnp.int32, (rows, exp_w), 0) % nb
    col_b = lax.broadcasted_iota(jnp.int32, (rows, exp_w), 1) // dk
    batch_mask = row_b == col_b
    batch_mask_nb = (lax.broadcasted_iota(jnp.int32, (nb, exp_w), 0)
                     == lax.broadcasted_iota(jnp.int32, (nb, exp_w), 1) // dk)

    v_t = v.T.astype(BF16)
    hr = range(heads)
    v_th = [v_t[h * HEAD_DV:(h + 1) * HEAD_DV, :] for h in hr]
    q_views = [_head_lanes(q_dec, h, dk) for h in hr]
    scores = [_mm_nt(q_views[h][1], _head_lanes(k_inv, h, dk)[1]) for h in hr]
    between_heads()
    q_exp = [jnp.where(batch_mask, jnp.concatenate([q_views[h][0]] * reps, axis=1), 0.0).astype(BF16)
             for h in hr]
    scores = [jnp.where(causal_same_batch, s, 0.0).astype(BF16) for s in scores]
    st = [st_ref[h] for h in hr]
    o_t = [_mm_nt(jnp.concatenate([v_th[h], st[h].astype(BF16)], axis=1),
                  jnp.concatenate([scores[h], q_exp[h]], axis=1)) for h in hr]
    between_heads()
    k_exp = [jnp.where(batch_mask, jnp.concatenate([_head_lanes(k_end, h, dk)[0]] * reps, axis=1),
                       0.0).astype(BF16) for h in hr]
    for h in hr:
        gl_dup = _head_lanes(g_last, h, dk)[0]
        gl_flat = jnp.sum(jnp.where(batch_mask_nb, jnp.concatenate([gl_dup] * reps, axis=1), 0.0),
                          axis=0, keepdims=True)
        st_ref[h] = st[h] * jnp.exp(gl_flat) + _mm(v_th[h], k_exp[h])
    between_heads()
    outs_t = [o * lax.rsqrt(jnp.mean(o * o, axis=0, keepdims=True) + EPS) for o in o_t]
    out = jnp.concatenate(outs_t, axis=0).T * norm_g
    between_heads()
    return out


def _layer_kernel(h_ref, norm_g_ref,
                  w_gla_qkv_ref, w_gla_lr_ref, w_gla_gate_ref, gla_w_lr_ref, gla_b_lr_ref, gla_norm_ref,
                  w_hg_ref, hg_lb_ref, hg_norm_ref,
                  w_s5_ref, s5_bw_ref, s5_cw_ref, s5_are_ref, s5_aim_ref, s5_d_ref, s5_glu_w_ref,
                  s5_glu_b_ref,
                  w_lru_ref, lru_conv_w_ref, lru_conv_b_ref, lru_wax_ref, lru_bax_ref, lru_lam_ref,
                  w_mg_ref, w_branch_ref, w_out_ref, final_norm_ref,
                  out_ref,
                  gla_st_ref, hg_st_ref, s5_scan_ref, s5_st_ref, lru_st_ref, conv_st_ref,
                  *, nb, final):
    @pl.when(pl.program_id(0) == 0)
    def _init():
        gla_st_ref[...] = jnp.zeros_like(gla_st_ref)
        hg_st_ref[...] = jnp.zeros_like(hg_st_ref)
        s5_st_ref[...] = jnp.zeros_like(s5_st_ref)
        lru_st_ref[...] = jnp.zeros_like(lru_st_ref)
        conv_st_ref[...] = jnp.zeros_like(conv_st_ref)

    h = h_ref[...]
    rows, d_model = h.shape
    steps = rows // nb
    xn = _rms_norm(h, norm_g_ref[...]).astype(BF16)

    gate_cols = 2 * LANES
    gate_chunks = []

    def emit_gate_chunks(n):
        for _ in range(n):
            c = len(gate_chunks)
            if c * gate_cols < N_BRANCH * d_model:
                gate_chunks.append(_sigmoid(_mm(xn, w_mg_ref[:, c * gate_cols:(c + 1) * gate_cols])))

    def branch_gate(n):
        per = d_model // gate_cols
        emit_gate_chunks((n + 1) * per - len(gate_chunks))
        return jnp.concatenate(gate_chunks[n * per:(n + 1) * per], axis=1)

    qkv = _mm(xn, w_gla_qkv_ref[...])
    w_q = GLA_HEADS * GLA_DK
    lr = _mm(xn, w_gla_lr_ref[...])
    logit = _mm(lr, gla_w_lr_ref[...]) + gla_b_lr_ref[...]
    hz = _mm(xn, w_hg_ref[...])
    log_a = -_softplus(-logit) * (1.0 / GLA_TAU)

    y_a = _chunk_attention(qkv[:, :w_q] * (GLA_DK ** -0.5), qkv[:, w_q:2 * w_q], qkv[:, 2 * w_q:],
                           log_a, gla_st_ref, gla_norm_ref[...], GLA_DK, GLA_HEADS, nb,
                           lambda: emit_gate_chunks(1))
    y_a = y_a * _silu(_mm(xn, w_gla_gate_ref[...]))
    merged = branch_gate(0) * _mm(y_a, w_branch_ref[0])

    sz = _mm(xn, w_s5_ref[...])
    w_h = HG_HEADS * HG_DK
    lb = hg_lb_ref[...]
    f = lb + (1.0 - lb) * _sigmoid(hz[:, w_h:2 * w_h])
    y_b = _chunk_attention(hz[:, :w_h], 1.0 - f, hz[:, 2 * w_h:3 * w_h], jnp.log(f),
                           hg_st_ref, hg_norm_ref[...], HG_DK, HG_HEADS, nb,
                           lambda: emit_gate_chunks(1))
    y_b = y_b * _silu(hz[:, 3 * w_h:])
    merged = merged + branch_gate(1) * _mm(y_b, w_branch_ref[1])

    u = sz[:, :BRANCH_W]
    half = S5_BLOCK_STATE // 2
    for j in range(S5_BLOCKS):
        s5_scan_ref[j] = _mm(u[:, j * LANES:(j + 1) * LANES], s5_bw_ref[j])
    lz = _mm(xn, w_lru_ref[...])
    y_blocks = []
    for j in range(S5_BLOCKS):
        a_re = jnp.broadcast_to(s5_are_ref[j:j + 1, :], (nb, half))
        a_im = jnp.broadcast_to(s5_aim_ref[j:j + 1, :], (nb, half))
        s_re = s5_st_ref[j, :, :half]
        s_im = s5_st_ref[j, :, half:]
        emit_gate_chunks(1)
        for t in range(steps):
            b_re = s5_scan_ref[j, t * nb:(t + 1) * nb, :half]
            b_im = s5_scan_ref[j, t * nb:(t + 1) * nb, half:]
            s_re, s_im = (a_re * s_re - a_im * s_im + b_re, a_re * s_im + a_im * s_re + b_im)
            s5_scan_ref[j, t * nb:(t + 1) * nb, :half] = s_re
            s5_scan_ref[j, t * nb:(t + 1) * nb, half:] = s_im
        s5_st_ref[j, :, :half] = s_re
        s5_st_ref[j, :, half:] = s_im
        y_blocks.append(_mm_nt(s5_cw_ref[j], s5_scan_ref[j]))
    y_c = jnp.concatenate(y_blocks, axis=0).T + s5_d_ref[...] * u
    y_c = _gelu_tanh(y_c)
    y_c = y_c * _sigmoid(_mm(y_c, s5_glu_w_ref[...]) + s5_glu_b_ref[...])
    y_c = y_c * _silu(sz[:, BRANCH_W:])
    merged = merged + branch_gate(2) * _mm(y_c, w_branch_ref[2])

    lx = lz[:, :BRANCH_W]
    hist = (CONV_W - 1) * nb
    x_full = jnp.concatenate([conv_st_ref[...], lx], axis=0)
    conv_st_ref[...] = lx[rows - hist:, :]
    xc = lru_conv_b_ref[...]
    for j in range(CONV_W):
        xc = xc + x_full[hist - j * nb: hist - j * nb + rows, :] * lru_conv_w_ref[j:j + 1, :]
    ax = _mm(xc, lru_wax_ref[...]) + lru_bax_ref[...]
    r_g = _sigmoid(ax[:, :BRANCH_W])
    i_g = _sigmoid(ax[:, BRANCH_W:])
    log_al = (-LRU_C) * r_g * _softplus(-lru_lam_ref[...])
    a_l = jnp.exp(log_al)
    inp = jnp.sqrt(1.0 - jnp.exp(2.0 * log_al)) * (i_g * xc)
    s_l = lru_st_ref[...]
    hs = []
    for t in range(steps):
        s_l = a_l[t * nb:(t + 1) * nb, :] * s_l + inp[t * nb:(t + 1) * nb, :]
        hs.append(s_l)
    lru_st_ref[...] = s_l
    y_d = jnp.concatenate(hs, axis=0) * _silu(lz[:, BRANCH_W:])
    merged = merged + branch_gate(3) * _mm(y_d, w_branch_ref[3])

    out = h + _mm(merged, w_out_ref[...])
    if final:
        out = _rms_norm(out, final_norm_ref[...])
    out_ref[...] = out


def _resident(arr):
    nd = arr.ndim
    return pl.BlockSpec(arr.shape, lambda i, _nd=nd: (0,) * _nd, pipeline_mode=pl.Buffered(1))


def _layer_call(h, params, final_norm, *, nb, final):
    n_rows, d_model = h.shape
    rows = STEPS * nb
    weights = list(params) + [final_norm]
    kern = functools.partial(_layer_kernel, nb=nb, final=final)
    return pl.pallas_call(
        kern,
        grid=(n_rows // rows,),
        in_specs=[pl.BlockSpec((rows, d_model), lambda i: (i, 0))] + [_resident(w) for w in weights],
        out_specs=pl.BlockSpec((rows, d_model), lambda i: (i, 0)),
        out_shape=jax.ShapeDtypeStruct((n_rows, d_model), F32),
        scratch_shapes=[
            pltpu.VMEM((GLA_HEADS, HEAD_DV, nb * GLA_DK), F32),
            pltpu.VMEM((HG_HEADS, HEAD_DV, nb * HG_DK), F32),
            pltpu.VMEM((S5_BLOCKS, rows, S5_BLOCK_STATE), F32),
            pltpu.VMEM((S5_BLOCKS, nb, S5_BLOCK_STATE), F32),
            pltpu.VMEM((nb, BRANCH_W), F32),
            pltpu.VMEM(((CONV_W - 1) * nb, BRANCH_W), F32),
        ],
        compiler_params=pltpu.CompilerParams(dimension_semantics=("arbitrary",),
                                             vmem_limit_bytes=VMEM_LIMIT_BYTES),
        name="hybrid_layer_final" if final else "hybrid_layer",
    )(h, *weights)


def _block_diag(blocks):
    n, r, c = blocks.shape
    eye = jnp.eye(n, dtype=blocks.dtype)
    return (blocks[:, :, None, :] * eye[:, None, :, None]).reshape(n * r, n * c)


def _layer_params(l, lb, norm_g, w_in, w_branch, w_out, gla_w_lr, gla_b_lr, gla_norm, hg_norm,
                  s5_lambda_re, s5_lambda_im, s5_log_dt, s5_b_re, s5_b_im, s5_c_re, s5_c_im, s5_d,
                  s5_glu_w, s5_glu_b, lru_conv_w, lru_conv_b, lru_wa, lru_ba, lru_wx, lru_bx, lru_lambda):
    w = w_in[l]
    d_model = w.shape[0]
    sizes = (GLA_HEADS * GLA_DK, GLA_HEADS * GLA_DK, GLA_HEADS * HEAD_DV, 16, BRANCH_W,
             HG_HEADS * HG_DK, HG_HEADS * HG_DK, HG_HEADS * HEAD_DV, BRANCH_W,
             BRANCH_W, BRANCH_W, BRANCH_W, BRANCH_W, 4 * d_model)
    offs = [0]
    for s in sizes:
        offs.append(offs[-1] + s)
    col = lambda a, b: w[:, offs[a]:offs[b]].astype(BF16)
    rank = sizes[3]
    w_gla_lr = jnp.pad(w[:, offs[3]:offs[4]], ((0, 0), (0, LANES - rank))).astype(BF16)
    gla_w_lr_p = jnp.pad(gla_w_lr[l], ((0, LANES - rank), (0, 0))).astype(BF16)

    lr_, li_ = s5_lambda_re[l].astype(F32), s5_lambda_im[l].astype(F32)
    dt = jnp.exp(s5_log_dt[l].astype(F32))[:, None]
    mag = jnp.exp(lr_ * dt)
    ab_re, ab_im = mag * jnp.cos(li_ * dt), mag * jnp.sin(li_ * dt)
    nr, ni = ab_re - 1.0, ab_im
    den = lr_ * lr_ + li_ * li_
    cr, ci = (nr * lr_ + ni * li_) / den, (ni * lr_ - nr * li_) / den
    br, bi = s5_b_re[l].astype(F32), s5_b_im[l].astype(F32)
    bb_re = cr[..., None] * br - ci[..., None] * bi
    bb_im = cr[..., None] * bi + ci[..., None] * br
    gpb = LANES // S5_GROUP

    def in_block(bb):
        t = jnp.swapaxes(bb, 1, 2).reshape(S5_BLOCKS, gpb, S5_GROUP, S5_STATE)
        return jnp.stack([_block_diag(t[j]) for j in range(S5_BLOCKS)])

    def out_block(c):
        t = c.astype(F32).reshape(S5_BLOCKS, gpb, S5_GROUP, S5_STATE)
        return jnp.stack([_block_diag(t[j]) for j in range(S5_BLOCKS)])

    s5_bw = jnp.concatenate([in_block(bb_re), in_block(bb_im)], axis=2).astype(BF16)
    s5_cw = jnp.concatenate([out_block(s5_c_re[l]), -out_block(s5_c_im[l])], axis=2).astype(BF16)
    s5_are = ab_re.reshape(S5_BLOCKS, gpb * S5_STATE)
    s5_aim = ab_im.reshape(S5_BLOCKS, gpb * S5_STATE)

    lru_wax = jnp.concatenate([_block_diag(lru_wa[l]), _block_diag(lru_wx[l])], axis=1).astype(BF16)
    lru_bax = jnp.concatenate([lru_ba[l], lru_bx[l]])[None, :]
    row = lambda a: a[None, :].astype(F32)
    return [
        row(norm_g[l]),
        jnp.concatenate([col(0, 1), col(1, 2), col(2, 3)], axis=1), w_gla_lr, col(4, 5),
        gla_w_lr_p, row(gla_b_lr[l]), gla_norm[l].astype(F32).reshape(1, -1),
        jnp.concatenate([col(5, 6), col(6, 7), col(7, 8), col(8, 9)], axis=1), row(lb),
        hg_norm[l].astype(F32).reshape(1, -1),
        jnp.concatenate([col(9, 10), col(10, 11)], axis=1), s5_bw, s5_cw, s5_are, s5_aim, row(s5_d[l]),
        s5_glu_w[l].astype(BF16), row(s5_glu_b[l]),
        jnp.concatenate([col(11, 12), col(12, 13)], axis=1), lru_conv_w[l].astype(F32), row(lru_conv_b[l]),
        lru_wax, lru_bax.astype(F32), row(lru_lambda[l]),
        col(13, 14), w_branch[l].astype(BF16), w_out[l].astype(BF16),
    ]


def kernel(x, meta_tokens, hgrn_lb_logits, final_norm, norm_g, w_in, w_branch, w_out, gla_w_lr, gla_b_lr, gla_norm, hg_norm, s5_lambda_re, s5_lambda_im, s5_log_dt, s5_b_re, s5_b_im, s5_c_re, s5_c_im, s5_d, s5_glu_w, s5_glu_b, lru_conv_w, lru_conv_b, lru_wa, lru_ba, lru_wx, lru_bx, lru_lambda):
    nb, seq, d_model = x.shape
    depth = w_in.shape[0]
    t_len = N_META + seq
    t_pad = -(-t_len // STEPS) * STEPS
    meta = jnp.broadcast_to(meta_tokens.astype(x.dtype)[:, None, :], (N_META, nb, d_model))
    h = jnp.concatenate([meta, jnp.swapaxes(x, 0, 1),
                         jnp.zeros((t_pad - t_len, nb, d_model), x.dtype)], axis=0)
    h = h.reshape(t_pad * nb, d_model)

    p = jax.nn.softmax(hgrn_lb_logits.astype(F32), axis=0)
    lb_all = jnp.cumsum(p, axis=0) - p[0:1]
    fin = final_norm[None, :].astype(F32)
    for l in range(depth):
        params = _layer_params(l, lb_all[l], norm_g, w_in, w_branch, w_out, gla_w_lr, gla_b_lr, gla_norm,
                               hg_norm, s5_lambda_re, s5_lambda_im, s5_log_dt, s5_b_re, s5_b_im, s5_c_re,
                               s5_c_im, s5_d, s5_glu_w, s5_glu_b, lru_conv_w, lru_conv_b, lru_wa, lru_ba,
                               lru_wx, lru_bx, lru_lambda)
        h = _layer_call(h, params, fin, nb=nb, final=(l == depth - 1))
    out = h.reshape(t_pad, nb, d_model)[N_META:t_len]
    return jnp.swapaxes(out, 0, 1)
```

```python
import functools
import math

import jax
import jax.numpy as jnp
from jax import lax
from jax.experimental import pallas as pl
from jax.experimental.pallas import tpu as pltpu

F32 = jnp.float32
BF16 = jnp.bfloat16

N_META = 16
EPS = 1e-6
STEPS = 32
LANES = 128
HEAD_DV = 128
BRANCH_W = 512
GLA_HEADS, GLA_DK, GLA_TAU = 4, 64, 16.0
HG_HEADS, HG_DK = 4, 128
S5_GROUP, S5_STATE = 16, 64
S5_BLOCKS = 4
S5_BLOCK_STATE = 2 * 8 * S5_STATE
N_BRANCH = 4
LRU_BLOCKS, LRU_C, CONV_W = 8, 8.0, 4
VMEM_LIMIT_BYTES = 58 * 1024 * 1024


def _sigmoid(x):
    return 1.0 / (1.0 + jnp.exp(-x))


def _softplus(x):
    return jnp.maximum(x, 0.0) + jnp.log1p(jnp.exp(-jnp.abs(x)))


def _silu(x):
    return x * _sigmoid(x)


def _gelu_tanh(x):
    c = math.sqrt(2.0 / math.pi)
    return 0.5 * x * (1.0 + jnp.tanh(c * (x + 0.044715 * (x * x * x))))


def _rms_norm(x, gain):
    return x * lax.rsqrt(jnp.mean(x * x, axis=-1, keepdims=True) + EPS) * gain


def _mm(a, w):
    return jnp.dot(a.astype(BF16), w, preferred_element_type=F32)


def _mm_nt(a, b):
    return lax.dot_general(a.astype(BF16), b.astype(BF16), (((1,), (1,)), ((), ())),
                           preferred_element_type=F32)


def _head_lanes(arr, h, dk):
    p = (h * dk) // LANES
    blk = arr[:, p * LANES:(p + 1) * LANES]
    if dk == LANES:
        return blk, blk
    assert dk * 2 == LANES
    lane = lax.broadcasted_iota(jnp.int32, blk.shape, 1)
    in_head = (lane < dk) if (h % 2 == 0) else (lane >= dk)
    swapped = pltpu.roll(blk, dk, axis=1)
    return jnp.where(in_head, blk, swapped), jnp.where(in_head, blk, 0.0)


def _chunk_attention(q, k, v, g, st_ref, norm_g, dk, heads, nb, between_heads):
    rows, width = q.shape
    steps = rows // nb
    g3 = g.reshape(steps, nb, width)
    acc = g3[0]
    cum = [acc]
    for t in range(1, steps):
        acc = acc + g3[t]
        cum.append(acc)
    g_cum = jnp.concatenate(cum, axis=0)
    g_last = acc
    g_last_rows = jnp.concatenate([g_last] * steps, axis=0)
    q_dec = q * jnp.exp(g_cum)
    k_inv = k * jnp.exp(-g_cum)
    k_end = k * jnp.exp(g_last_rows - g_cum)

    exp_w = nb * dk
    reps = exp_w // LANES
    r_i = lax.broadcasted_iota(jnp.int32, (rows, rows), 0)
    c_i = lax.broadcasted_iota(jnp.int32, (rows, rows), 1)
    causal_same_batch = ((r_i % nb) == (c_i % nb)) & ((c_i // nb) <= (r_i // nb))
    row_b = lax.broadcasted_iota(jnp.int32, (rows, exp_w), 0) % nb
    col_b = lax.broadcasted_iota(jnp.int32, (rows, exp_w), 1) // dk
    batch_mask = row_b == col_b
    batch_mask_nb = (lax.broadcasted_iota(jnp.int32, (nb, exp_w), 0)
                     == lax.broadcasted_iota(jnp.int32, (nb, exp_w), 1) // dk)

    v_t = v.T.astype(BF16)
    hr = range(heads)
    v_th = [v_t[h * HEAD_DV:(h + 1) * HEAD_DV, :] for h in hr]
    q_views = [_head_lanes(q_dec, h, dk) for h in hr]
    scores = [_mm_nt(q_views[h][1], _head_lanes(k_inv, h, dk)[1]) for h in hr]
    between_heads()
    q_exp = [jnp.where(batch_mask, jnp.concatenate([q_views[h][0]] * reps, axis=1), 0.0).astype(BF16)
             for h in hr]
    scores = [jnp.where(causal_same_batch, s, 0.0).astype(BF16) for s in scores]
    st = [st_ref[h] for h in hr]
    o_t = [_mm_nt(jnp.concatenate([v_th[h], st[h].astype(BF16)], axis=1),
                  jnp.concatenate([scores[h], q_exp[h]], axis=1)) for h in hr]
    between_heads()
    k_exp = [jnp.where(batch_mask, jnp.concatenate([_head_lanes(k_end, h, dk)[0]] * reps, axis=1),
                       0.0).astype(BF16) for h in hr]
    for h in hr:
        gl_dup = _head_lanes(g_last, h, dk)[0]
        gl_flat = jnp.sum(jnp.where(batch_mask_nb, jnp.concatenate([gl_dup] * reps, axis=1), 0.0),
                          axis=0, keepdims=True)
        st_ref[h] = st[h] * jnp.exp(gl_flat) + _mm(v_th[h], k_exp[h])
    between_heads()
    outs_t = [o * lax.rsqrt(jnp.mean(o * o, axis=0, keepdims=True) + EPS) for o in o_t]
    out = jnp.concatenate(outs_t, axis=0).T * norm_g
    between_heads()
    return out


def _layer_kernel(h_ref, head_ref, norm_g_ref,
                  w_gla_qkv_ref, w_gla_lr_ref, w_gla_gate_ref, gla_w_lr_ref, gla_b_lr_ref, gla_norm_ref,
                  w_hg_ref, hg_lb_ref, hg_norm_ref,
                  w_s5_ref, s5_bw_ref, s5_cw_ref, s5_are_ref, s5_aim_ref, s5_d_ref, s5_glu_w_ref,
                  s5_glu_b_ref,
                  w_lru_ref, lru_conv_w_ref, lru_conv_b_ref, lru_wax_ref, lru_bax_ref, lru_lam_ref,
                  w_mg_ref, w_branch_ref, w_out_ref, final_norm_ref,
                  out_ref,
                  gla_st_ref, hg_st_ref, s5_scan_ref, s5_st_ref, lru_st_ref, conv_st_ref, relayout_ref,
                  *, nb, first, final):
    step_idx = pl.program_id(0)

    @pl.when(step_idx == 0)
    def _init():
        gla_st_ref[...] = jnp.zeros_like(gla_st_ref)
        hg_st_ref[...] = jnp.zeros_like(hg_st_ref)
        s5_st_ref[...] = jnp.zeros_like(s5_st_ref)
        lru_st_ref[...] = jnp.zeros_like(lru_st_ref)
        conv_st_ref[...] = jnp.zeros_like(conv_st_ref)

    rows, d_model = head_ref.shape
    steps = rows // nb
    lane_slabs = d_model // LANES
    if first:
        for b in range(nb):
            for c in range(lane_slabs):
                relayout_ref[c, pl.ds(b, steps, stride=nb), :] = h_ref[b, :, c * LANES:(c + 1) * LANES]
        h_seq = jnp.concatenate([relayout_ref[c] for c in range(lane_slabs)], axis=1)
        h = jnp.where(step_idx == 0, head_ref[...], h_seq)
    else:
        h = h_ref[...]
    xn = _rms_norm(h, norm_g_ref[...]).astype(BF16)

    gate_cols = 2 * LANES
    gate_chunks = []

    def emit_gate_chunks(n):
        for _ in range(n):
            c = len(gate_chunks)
            if c * gate_cols < N_BRANCH * d_model:
                gate_chunks.append(_sigmoid(_mm(xn, w_mg_ref[:, c * gate_cols:(c + 1) * gate_cols])))

    def branch_gate(n):
        per = d_model // gate_cols
        emit_gate_chunks((n + 1) * per - len(gate_chunks))
        return jnp.concatenate(gate_chunks[n * per:(n + 1) * per], axis=1)

    qkv = _mm(xn, w_gla_qkv_ref[...])
    w_q = GLA_HEADS * GLA_DK
    lr = _mm(xn, w_gla_lr_ref[...])
    logit = _mm(lr, gla_w_lr_ref[...]) + gla_b_lr_ref[...]
    hz = _mm(xn, w_hg_ref[...])
    log_a = -_softplus(-logit) * (1.0 / GLA_TAU)

    y_a = _chunk_attention(qkv[:, :w_q] * (GLA_DK ** -0.5), qkv[:, w_q:2 * w_q], qkv[:, 2 * w_q:],
                           log_a, gla_st_ref, gla_norm_ref[...], GLA_DK, GLA_HEADS, nb,
                           lambda: emit_gate_chunks(1))
    y_a = y_a * _silu(_mm(xn, w_gla_gate_ref[...]))
    merged = branch_gate(0) * _mm(y_a, w_branch_ref[0])

    sz = _mm(xn, w_s5_ref[...])
    w_h = HG_HEADS * HG_DK
    lb = hg_lb_ref[...]
    f = lb + (1.0 - lb) * _sigmoid(hz[:, w_h:2 * w_h])
    y_b = _chunk_attention(hz[:, :w_h], 1.0 - f, hz[:, 2 * w_h:3 * w_h], jnp.log(f),
                           hg_st_ref, hg_norm_ref[...], HG_DK, HG_HEADS, nb,
                           lambda: emit_gate_chunks(1))
    y_b = y_b * _silu(hz[:, 3 * w_h:])
    merged = merged + branch_gate(1) * _mm(y_b, w_branch_ref[1])

    u = sz[:, :BRANCH_W]
    half = S5_BLOCK_STATE // 2
    for j in range(S5_BLOCKS):
        s5_scan_ref[j] = _mm(u[:, j * LANES:(j + 1) * LANES], s5_bw_ref[j])
    lz = _mm(xn, w_lru_ref[...])
    y_blocks = []
    for j in range(S5_BLOCKS):
        a_re = jnp.broadcast_to(s5_are_ref[j:j + 1, :], (nb, half))
        a_im = jnp.broadcast_to(s5_aim_ref[j:j + 1, :], (nb, half))
        s_re = s5_st_ref[j, :, :half]
        s_im = s5_st_ref[j, :, half:]
        emit_gate_chunks(1)
        for t in range(steps):
            b_re = s5_scan_ref[j, t * nb:(t + 1) * nb, :half]
            b_im = s5_scan_ref[j, t * nb:(t + 1) * nb, half:]
            s_re, s_im = (a_re * s_re - a_im * s_im + b_re, a_re * s_im + a_im * s_re + b_im)
            s5_scan_ref[j, t * nb:(t + 1) * nb, :half] = s_re
            s5_scan_ref[j, t * nb:(t + 1) * nb, half:] = s_im
        s5_st_ref[j, :, :half] = s_re
        s5_st_ref[j, :, half:] = s_im
        y_blocks.append(_mm_nt(s5_cw_ref[j], s5_scan_ref[j]))
    y_c = jnp.concatenate(y_blocks, axis=0).T + s5_d_ref[...] * u
    y_c = _gelu_tanh(y_c)
    y_c = y_c * _sigmoid(_mm(y_c, s5_glu_w_ref[...]) + s5_glu_b_ref[...])
    y_c = y_c * _silu(sz[:, BRANCH_W:])
    merged = merged + branch_gate(2) * _mm(y_c, w_branch_ref[2])

    lx = lz[:, :BRANCH_W]
    hist = (CONV_W - 1) * nb
    x_full = jnp.concatenate([conv_st_ref[...], lx], axis=0)
    conv_st_ref[...] = lx[rows - hist:, :]
    xc = lru_conv_b_ref[...]
    for j in range(CONV_W):
        xc = xc + x_full[hist - j * nb: hist - j * nb + rows, :] * lru_conv_w_ref[j:j + 1, :]
    ax = _mm(xc, lru_wax_ref[...]) + lru_bax_ref[...]
    r_g = _sigmoid(ax[:, :BRANCH_W])
    i_g = _sigmoid(ax[:, BRANCH_W:])
    log_al = (-LRU_C) * r_g * _softplus(-lru_lam_ref[...])
    a_l = jnp.exp(log_al)
    inp = jnp.sqrt(1.0 - jnp.exp(2.0 * log_al)) * (i_g * xc)
    pad_rows = (STEPS - N_META) * nb
    row_id = step_idx * rows + lax.broadcasted_iota(jnp.int32, inp.shape, 0)
    inp = jnp.where(row_id >= pad_rows, inp, 0.0)
    s_l = lru_st_ref[...]
    hs = []
    for t in range(steps):
        s_l = a_l[t * nb:(t + 1) * nb, :] * s_l + inp[t * nb:(t + 1) * nb, :]
        hs.append(s_l)
    lru_st_ref[...] = s_l
    y_d = jnp.concatenate(hs, axis=0) * _silu(lz[:, BRANCH_W:])
    merged = merged + branch_gate(3) * _mm(y_d, w_branch_ref[3])

    out = h + _mm(merged, w_out_ref[...])
    if final:
        out = _rms_norm(out, final_norm_ref[...])
        for c in range(lane_slabs):
            relayout_ref[c] = out[:, c * LANES:(c + 1) * LANES]
        for b in range(nb):
            for c in range(lane_slabs):
                out_ref[b, :, c * LANES:(c + 1) * LANES] = relayout_ref[c, pl.ds(b, steps, stride=nb), :]
    else:
        out_ref[...] = out


def _resident(arr):
    nd = arr.ndim
    return pl.BlockSpec(arr.shape, lambda i, _nd=nd: (0,) * _nd, pipeline_mode=pl.Buffered(1))


def _layer_call(h, head, params, final_norm, *, nb, seq, first, final):
    rows, d_model = head.shape
    n_blocks = 1 + seq // STEPS
    weights = [head] + list(params) + [final_norm]
    kern = functools.partial(_layer_kernel, nb=nb, first=first, final=final)
    time_major = pl.BlockSpec((rows, d_model), lambda i: (i, 0))
    batch_major = pl.BlockSpec((nb, STEPS, d_model), lambda i: (0, jnp.maximum(i - 1, 0), 0))
    return pl.pallas_call(
        kern,
        grid=(n_blocks,),
        in_specs=[batch_major if first else time_major] + [_resident(w) for w in weights],
        out_specs=batch_major if final else time_major,
        out_shape=jax.ShapeDtypeStruct((nb, seq, d_model) if final else (n_blocks * rows, d_model), F32),
        scratch_shapes=[
            pltpu.VMEM((GLA_HEADS, HEAD_DV, nb * GLA_DK), F32),
            pltpu.VMEM((HG_HEADS, HEAD_DV, nb * HG_DK), F32),
            pltpu.VMEM((S5_BLOCKS, rows, S5_BLOCK_STATE), F32),
            pltpu.VMEM((S5_BLOCKS, nb, S5_BLOCK_STATE), F32),
            pltpu.VMEM((nb, BRANCH_W), F32),
            pltpu.VMEM(((CONV_W - 1) * nb, BRANCH_W), F32),
            pltpu.VMEM((d_model // LANES, rows, LANES), F32),
        ],
        compiler_params=pltpu.CompilerParams(dimension_semantics=("arbitrary",),
                                             vmem_limit_bytes=VMEM_LIMIT_BYTES),
        name="hybrid_layer_final" if final else "hybrid_layer",
    )(h, *weights)


def _block_diag(blocks):
    n, r, c = blocks.shape
    eye = jnp.eye(n, dtype=blocks.dtype)
    return (blocks[:, :, None, :] * eye[:, None, :, None]).reshape(n * r, n * c)


def _layer_params(l, lb, norm_g, w_in, w_branch, w_out, gla_w_lr, gla_b_lr, gla_norm, hg_norm,
                  s5_lambda_re, s5_lambda_im, s5_log_dt, s5_b_re, s5_b_im, s5_c_re, s5_c_im, s5_d,
                  s5_glu_w, s5_glu_b, lru_conv_w, lru_conv_b, lru_wa, lru_ba, lru_wx, lru_bx, lru_lambda):
    w = w_in[l]
    d_model = w.shape[0]
    sizes = (GLA_HEADS * GLA_DK, GLA_HEADS * GLA_DK, GLA_HEADS * HEAD_DV, 16, BRANCH_W,
             HG_HEADS * HG_DK, HG_HEADS * HG_DK, HG_HEADS * HEAD_DV, BRANCH_W,
             BRANCH_W, BRANCH_W, BRANCH_W, BRANCH_W, 4 * d_model)
    offs = [0]
    for s in sizes:
        offs.append(offs[-1] + s)
    col = lambda a, b: w[:, offs[a]:offs[b]].astype(BF16)
    rank = sizes[3]
    w_gla_lr = jnp.pad(w[:, offs[3]:offs[4]], ((0, 0), (0, LANES - rank))).astype(BF16)
    gla_w_lr_p = jnp.pad(gla_w_lr[l], ((0, LANES - rank), (0, 0))).astype(BF16)

    lr_, li_ = s5_lambda_re[l].astype(F32), s5_lambda_im[l].astype(F32)
    dt = jnp.exp(s5_log_dt[l].astype(F32))[:, None]
    mag = jnp.exp(lr_ * dt)
    ab_re, ab_im = mag * jnp.cos(li_ * dt), mag * jnp.sin(li_ * dt)
    nr, ni = ab_re - 1.0, ab_im
    den = lr_ * lr_ + li_ * li_
    cr, ci = (nr * lr_ + ni * li_) / den, (ni * lr_ - nr * li_) / den
    br, bi = s5_b_re[l].astype(F32), s5_b_im[l].astype(F32)
    bb_re = cr[..., None] * br - ci[..., None] * bi
    bb_im = cr[..., None] * bi + ci[..., None] * br
    gpb = LANES // S5_GROUP

    def in_block(bb):
        t = jnp.swapaxes(bb, 1, 2).reshape(S5_BLOCKS, gpb, S5_GROUP, S5_STATE)
        return jnp.stack([_block_diag(t[j]) for j in range(S5_BLOCKS)])

    def out_block(c):
        t = c.astype(F32).reshape(S5_BLOCKS, gpb, S5_GROUP, S5_STATE)
        return jnp.stack([_block_diag(t[j]) for j in range(S5_BLOCKS)])

    s5_bw = jnp.concatenate([in_block(bb_re), in_block(bb_im)], axis=2).astype(BF16)
    s5_cw = jnp.concatenate([out_block(s5_c_re[l]), -out_block(s5_c_im[l])], axis=2).astype(BF16)
    s5_are = ab_re.reshape(S5_BLOCKS, gpb * S5_STATE)
    s5_aim = ab_im.reshape(S5_BLOCKS, gpb * S5_STATE)

    lru_wax = jnp.concatenate([_block_diag(lru_wa[l]), _block_diag(lru_wx[l])], axis=1).astype(BF16)
    lru_bax = jnp.concatenate([lru_ba[l], lru_bx[l]])[None, :]
    row = lambda a: a[None, :].astype(F32)
    return [
        row(norm_g[l]),
        jnp.concatenate([col(0, 1), col(1, 2), col(2, 3)], axis=1), w_gla_lr, col(4, 5),
        gla_w_lr_p, row(gla_b_lr[l]), gla_norm[l].astype(F32).reshape(1, -1),
        jnp.concatenate([col(5, 6), col(6, 7), col(7, 8), col(8, 9)], axis=1), row(lb),
        hg_norm[l].astype(F32).reshape(1, -1),
        jnp.concatenate([col(9, 10), col(10, 11)], axis=1), s5_bw, s5_cw, s5_are, s5_aim, row(s5_d[l]),
        s5_glu_w[l].astype(BF16), row(s5_glu_b[l]),
        jnp.concatenate([col(11, 12), col(12, 13)], axis=1), lru_conv_w[l].astype(F32), row(lru_conv_b[l]),
        lru_wax, lru_bax.astype(F32), row(lru_lambda[l]),
        col(13, 14), w_branch[l].astype(BF16), w_out[l].astype(BF16),
    ]


def kernel(x, meta_tokens, hgrn_lb_logits, final_norm, norm_g, w_in, w_branch, w_out, gla_w_lr, gla_b_lr, gla_norm, hg_norm, s5_lambda_re, s5_lambda_im, s5_log_dt, s5_b_re, s5_b_im, s5_c_re, s5_c_im, s5_d, s5_glu_w, s5_glu_b, lru_conv_w, lru_conv_b, lru_wa, lru_ba, lru_wx, lru_bx, lru_lambda):
    nb, seq, d_model = x.shape
    depth = w_in.shape[0]
    assert seq % STEPS == 0 and N_META <= STEPS and depth >= 2
    meta = jnp.broadcast_to(meta_tokens.astype(F32)[:, None, :], (N_META, nb, d_model))
    head = jnp.concatenate([jnp.zeros((STEPS - N_META, nb, d_model), F32), meta], axis=0)
    head = head.reshape(STEPS * nb, d_model)

    p = jax.nn.softmax(hgrn_lb_logits.astype(F32), axis=0)
    lb_all = jnp.cumsum(p, axis=0) - p[0:1]
    fin = final_norm[None, :].astype(F32)
    h = x.astype(F32)
    for l in range(depth):
        params = _layer_params(l, lb_all[l], norm_g, w_in, w_branch, w_out, gla_w_lr, gla_b_lr, gla_norm,
                               hg_norm, s5_lambda_re, s5_lambda_im, s5_log_dt, s5_b_re, s5_b_im, s5_c_re,
                               s5_c_im, s5_d, s5_glu_w, s5_glu_b, lru_conv_w, lru_conv_b, lru_wa, lru_ba,
                               lru_wx, lru_bx, lru_lambda)
        h = _layer_call(h, head, params, fin, nb=nb, seq=seq, first=(l == 0), final=(l == depth - 1))
    return h
```

```python
import functools
import math

import jax
import jax.numpy as jnp
from jax import lax
from jax.experimental import pallas as pl
from jax.experimental.pallas import tpu as pltpu

F32 = jnp.float32
BF16 = jnp.bfloat16

N_META = 16
EPS = 1e-6
STEPS = 32
LANES = 128
HEAD_DV = 128
BRANCH_W = 512
GLA_HEADS, GLA_DK, GLA_TAU = 4, 64, 16.0
HG_HEADS, HG_DK = 4, 128
S5_GROUP, S5_STATE = 16, 64
S5_BLOCKS = 4
S5_BLOCK_STATE = 2 * 8 * S5_STATE
N_BRANCH = 4
LRU_BLOCKS, LRU_C, CONV_W = 8, 8.0, 4
VMEM_LIMIT_BYTES = 58 * 1024 * 1024


def _sigmoid(x):
    return 0.5 * jnp.tanh(0.5 * x) + 0.5


def _softplus(x):
    return jnp.maximum(x, 0.0) + jnp.log1p(jnp.exp(-jnp.abs(x)))


def _sigmoid_exp(x):
    return 1.0 / (1.0 + jnp.exp(-x))


def _silu(x):
    return x * _sigmoid(x)


def _gelu_tanh(x):
    c = math.sqrt(2.0 / math.pi)
    return 0.5 * x * (1.0 + jnp.tanh(c * (x + 0.044715 * (x * x * x))))


def _rms_norm(x, gain):
    return x * lax.rsqrt(jnp.mean(x * x, axis=-1, keepdims=True) + EPS) * gain


def _mm(a, w):
    return jnp.dot(a.astype(BF16), w, preferred_element_type=F32)


def _mm_nt(a, b):
    return lax.dot_general(a.astype(BF16), b.astype(BF16), (((1,), (1,)), ((), ())),
                           preferred_element_type=F32)


def _head_dup(arr, h, dk):
    p = (h * dk) // LANES
    blk = arr[:, p * LANES:(p + 1) * LANES]
    if dk == LANES:
        return blk
    assert dk * 2 == LANES
    lane = lax.broadcasted_iota(jnp.int32, blk.shape, 1)
    in_head = (lane < dk) if (h % 2 == 0) else (lane >= dk)
    return jnp.where(in_head, blk, pltpu.roll(blk, dk, axis=1))


def _chunk_attention(q, k, v, g, st_ref, norm_g, dk, heads, nb, between_stages):
    rows, width = q.shape
    steps = rows // nb
    g3 = g.reshape(steps, nb, width)
    acc = g3[0]
    cum = [acc]
    for t in range(1, steps):
        acc = acc + g3[t]
        cum.append(acc)
    g_cum = jnp.concatenate(cum, axis=0)
    g_last = acc
    g_last_rows = jnp.concatenate([g_last] * steps, axis=0)
    q_dec = q * jnp.exp(g_cum)
    k_inv = k * jnp.exp(-g_cum)
    k_end = k * jnp.exp(g_last_rows - g_cum)

    exp_w = nb * dk
    reps = exp_w // LANES
    r_i = lax.broadcasted_iota(jnp.int32, (rows, rows), 0)
    c_i = lax.broadcasted_iota(jnp.int32, (rows, rows), 1)
    causal_same_batch_t = ((r_i % nb) == (c_i % nb)) & ((r_i // nb) <= (c_i // nb))
    batch_mask = (lax.broadcasted_iota(jnp.int32, (rows, exp_w), 0) % nb
                  == lax.broadcasted_iota(jnp.int32, (rows, exp_w), 1) // dk)
    batch_mask_t = (lax.broadcasted_iota(jnp.int32, (exp_w, rows), 0) // dk
                    == lax.broadcasted_iota(jnp.int32, (exp_w, rows), 1) % nb)
    batch_mask_nb = (lax.broadcasted_iota(jnp.int32, (nb, exp_w), 0)
                     == lax.broadcasted_iota(jnp.int32, (nb, exp_w), 1) // dk)

    v_t = v.T.astype(BF16)
    q_t = q_dec.T
    hr = range(heads)
    v_th = [v_t[h * HEAD_DV:(h + 1) * HEAD_DV, :] for h in hr]
    scores_t = []
    for h in hr:
        p = (h * dk) // LANES
        q_blk = q_t[p * LANES:(p + 1) * LANES, :]
        if dk < LANES:
            row = lax.broadcasted_iota(jnp.int32, q_blk.shape, 0)
            q_blk = jnp.where((row < dk) if (h % 2 == 0) else (row >= dk), q_blk, 0.0)
        scores_t.append(_mm(k_inv[:, p * LANES:(p + 1) * LANES], q_blk.astype(BF16)))
    between_stages()
    q_exp_t = [jnp.where(batch_mask_t, jnp.concatenate([q_t[h * dk:(h + 1) * dk, :]] * nb, axis=0),
                         0.0).astype(BF16) for h in hr]
    scores_t = [jnp.where(causal_same_batch_t, s, 0.0).astype(BF16) for s in scores_t]
    st = [st_ref[h] for h in hr]
    o_t = [_mm(jnp.concatenate([v_th[h], st[h].astype(BF16)], axis=1),
               jnp.concatenate([scores_t[h], q_exp_t[h]], axis=0)) for h in hr]
    between_stages()
    k_exp = [jnp.where(batch_mask, jnp.concatenate([_head_dup(k_end, h, dk)] * reps, axis=1),
                       0.0).astype(BF16) for h in hr]
    for h in hr:
        gl_dup = _head_dup(g_last, h, dk)
        gl_flat = jnp.sum(jnp.where(batch_mask_nb, jnp.concatenate([gl_dup] * reps, axis=1), 0.0),
                          axis=0, keepdims=True)
        st_ref[h] = st[h] * jnp.exp(gl_flat) + _mm(v_th[h], k_exp[h])
    between_stages()
    outs_t = [o * lax.rsqrt(jnp.mean(o * o, axis=0, keepdims=True) + EPS) for o in o_t]
    out = jnp.concatenate(outs_t, axis=0).T * norm_g
    between_stages()
    return out


def _layer_kernel(next_ref, head_ref, norm_g_ref,
                  w_gla_qkv_ref, w_gla_gate_ref, gla_w_lr_ref, gla_b_lr_ref, gla_norm_ref,
                  w_hg_ref, hg_lb_ref, hg_norm_ref,
                  w_s5_ref, s5_bw_ref, s5_cw_ref, s5_are_ref, s5_aim_ref, s5_d_ref, s5_glu_w_ref,
                  s5_glu_b_ref,
                  w_lru_ref, lru_conv_w_ref, lru_conv_b_ref, lru_wax_ref, lru_bax_ref, lru_lam_ref,
                  w_mg_ref, w_branch_ref, w_out_ref, final_norm_ref,
                  out_ref,
                  gla_st_ref, hg_st_ref, s5_scan_ref, s5_st_ref, lru_st_ref, conv_st_ref, relayout_ref,
                  h_buf_ref, xn_buf_ref,
                  *, nb, first, final):
    step_idx = pl.program_id(0)
    cur = step_idx % 2
    norm_g = norm_g_ref[...]

    @pl.when(step_idx == 0)
    def _init():
        gla_st_ref[...] = jnp.zeros_like(gla_st_ref)
        hg_st_ref[...] = jnp.zeros_like(hg_st_ref)
        s5_st_ref[...] = jnp.zeros_like(s5_st_ref)
        lru_st_ref[...] = jnp.zeros_like(lru_st_ref)
        conv_st_ref[...] = jnp.zeros_like(conv_st_ref)
        h_buf_ref[0] = head_ref[...]
        xn_buf_ref[0] = _rms_norm(head_ref[...], norm_g).astype(BF16)

    rows, d_model = head_ref.shape
    steps = rows // nb
    lane_slabs = d_model // LANES
    xn = xn_buf_ref[cur]

    def prepare_next_block():
        if first:
            for b in range(nb):
                for c in range(lane_slabs):
                    relayout_ref[c, pl.ds(b, steps, stride=nb), :] = next_ref[b, :, c * LANES:(c + 1) * LANES]
            h_next = jnp.concatenate([relayout_ref[c] for c in range(lane_slabs)], axis=1)
        else:
            h_next = next_ref[...]
        h_buf_ref[1 - cur] = h_next
        xn_buf_ref[1 - cur] = _rms_norm(h_next, norm_g).astype(BF16)

    gate_cols = 2 * LANES
    gate_chunks = []

    def emit_gate_chunks(n):
        for _ in range(n):
            c = len(gate_chunks)
            if c * gate_cols < N_BRANCH * d_model:
                gate_chunks.append(_sigmoid(_mm(xn, w_mg_ref[:, c * gate_cols:(c + 1) * gate_cols])))

    def branch_gate(n):
        per = d_model // gate_cols
        emit_gate_chunks((n + 1) * per - len(gate_chunks))
        return jnp.concatenate(gate_chunks[n * per:(n + 1) * per], axis=1)

    qkv = _mm(xn, w_gla_qkv_ref[...])
    w_q = GLA_HEADS * GLA_DK
    w_qkv = 2 * w_q + GLA_HEADS * HEAD_DV
    logit = _mm(qkv[:, w_qkv:], gla_w_lr_ref[...]) + gla_b_lr_ref[...]
    hz = _mm(xn, w_hg_ref[...])
    sz = _mm(xn, w_s5_ref[...])
    lz = _mm(xn, w_lru_ref[...])
    gla_gate = _mm(xn, w_gla_gate_ref[...])
    prepare_next_block()
    log_a = -_softplus(-logit) * (1.0 / GLA_TAU)

    lx = lz[:, :BRANCH_W]
    hist = (CONV_W - 1) * nb
    x_full = jnp.concatenate([conv_st_ref[...], lx], axis=0)
    conv_st_ref[...] = lx[rows - hist:, :]
    xc = lru_conv_b_ref[...]
    for j in range(CONV_W):
        xc = xc + x_full[hist - j * nb: hist - j * nb + rows, :] * lru_conv_w_ref[j:j + 1, :]
    ax = _mm(xc, lru_wax_ref[...]) + lru_bax_ref[...]
    r_g = _sigmoid(ax[:, :BRANCH_W])
    i_g = _sigmoid(ax[:, BRANCH_W:])
    log_al = (-LRU_C) * r_g * _softplus(-lru_lam_ref[...])
    a_l = jnp.exp(log_al)
    inp = jnp.sqrt(1.0 - jnp.exp(2.0 * log_al)) * (i_g * xc)
    pad_rows = (STEPS - N_META) * nb
    row_id = step_idx * rows + lax.broadcasted_iota(jnp.int32, inp.shape, 0)
    inp = jnp.where(row_id >= pad_rows, inp, 0.0)
    s_l = lru_st_ref[...]
    hs = []
    for t in range(steps):
        s_l = a_l[t * nb:(t + 1) * nb, :] * s_l + inp[t * nb:(t + 1) * nb, :]
        hs.append(s_l)
    lru_st_ref[...] = s_l
    y_d = jnp.concatenate(hs, axis=0) * _silu(lz[:, BRANCH_W:])

    y_a = _chunk_attention(qkv[:, :w_q] * (GLA_DK ** -0.5), qkv[:, w_q:2 * w_q], qkv[:, 2 * w_q:w_qkv],
                           log_a, gla_st_ref, gla_norm_ref[...], GLA_DK, GLA_HEADS, nb,
                           lambda: emit_gate_chunks(1))
    y_a = y_a * _silu(gla_gate)
    merged = branch_gate(0) * _mm(y_a, w_branch_ref[0])

    w_h = HG_HEADS * HG_DK
    lb = hg_lb_ref[...]
    f = lb + (1.0 - lb) * _sigmoid_exp(hz[:, w_h:2 * w_h])
    y_b = _chunk_attention(hz[:, :w_h], 1.0 - f, hz[:, 2 * w_h:3 * w_h], jnp.log(f),
                           hg_st_ref, hg_norm_ref[...], HG_DK, HG_HEADS, nb,
                           lambda: emit_gate_chunks(1))
    y_b = y_b * _silu(hz[:, 3 * w_h:])
    merged = merged + branch_gate(1) * _mm(y_b, w_branch_ref[1])

    u = sz[:, :BRANCH_W]
    half = S5_BLOCK_STATE // 2
    for j in range(S5_BLOCKS):
        s5_scan_ref[j] = _mm(u[:, j * LANES:(j + 1) * LANES], s5_bw_ref[j])
    y_blocks = []
    for j in range(S5_BLOCKS):
        a_re = jnp.broadcast_to(s5_are_ref[j:j + 1, :], (nb, half))
        a_im = jnp.broadcast_to(s5_aim_ref[j:j + 1, :], (nb, half))
        s_re = s5_st_ref[j, :, :half]
        s_im = s5_st_ref[j, :, half:]
        emit_gate_chunks(1)
        for t in range(steps):
            b_re = s5_scan_ref[j, t * nb:(t + 1) * nb, :half]
            b_im = s5_scan_ref[j, t * nb:(t + 1) * nb, half:]
            s_re, s_im = (a_re * s_re - a_im * s_im + b_re, a_re * s_im + a_im * s_re + b_im)
            s5_scan_ref[j, t * nb:(t + 1) * nb, :half] = s_re
            s5_scan_ref[j, t * nb:(t + 1) * nb, half:] = s_im
        s5_st_ref[j, :, :half] = s_re
        s5_st_ref[j, :, half:] = s_im
        y_blocks.append(_mm_nt(s5_cw_ref[j], s5_scan_ref[j]))
    y_c = jnp.concatenate(y_blocks, axis=0).T + s5_d_ref[...] * u
    y_c = _gelu_tanh(y_c)
    y_c = y_c * _sigmoid(_mm(y_c, s5_glu_w_ref[...]) + s5_glu_b_ref[...])
    y_c = y_c * _silu(sz[:, BRANCH_W:])
    merged = merged + branch_gate(2) * _mm(y_c, w_branch_ref[2])

    merged = merged + branch_gate(3) * _mm(y_d, w_branch_ref[3])

    out = h_buf_ref[cur] + _mm(merged, w_out_ref[...])
    if final:
        out = _rms_norm(out, final_norm_ref[...])
        for c in range(lane_slabs):
            relayout_ref[c] = out[:, c * LANES:(c + 1) * LANES]
        for b in range(nb):
            for c in range(lane_slabs):
                out_ref[b, :, c * LANES:(c + 1) * LANES] = relayout_ref[c, pl.ds(b, steps, stride=nb), :]
    else:
        out_ref[...] = out


def _resident(arr):
    nd = arr.ndim
    return pl.BlockSpec(arr.shape, lambda i, _nd=nd: (0,) * _nd, pipeline_mode=pl.Buffered(1))


def _layer_call(h, head, params, final_norm, *, nb, seq, first, final):
    rows, d_model = head.shape
    n_blocks = 1 + seq // STEPS
    weights = list(params) + [final_norm]
    kern = functools.partial(_layer_kernel, nb=nb, first=first, final=final)
    if first:
        next_spec = pl.BlockSpec((nb, STEPS, d_model), lambda i: (0, jnp.minimum(i, n_blocks - 2), 0))
        head_spec = _resident(head)
    else:
        next_spec = pl.BlockSpec((rows, d_model), lambda i: (jnp.minimum(i + 1, n_blocks - 1), 0))
        head_spec = pl.BlockSpec((rows, d_model), lambda i: (0, 0), pipeline_mode=pl.Buffered(1))
        head = h
    out_batch_major = pl.BlockSpec((nb, STEPS, d_model), lambda i: (0, jnp.maximum(i - 1, 0), 0))
    return pl.pallas_call(
        kern,
        grid=(n_blocks,),
        in_specs=[next_spec, head_spec] + [_resident(w) for w in weights],
        out_specs=out_batch_major if final else pl.BlockSpec((rows, d_model), lambda i: (i, 0)),
        out_shape=jax.ShapeDtypeStruct((nb, seq, d_model) if final else (n_blocks * rows, d_model), F32),
        scratch_shapes=[
            pltpu.VMEM((GLA_HEADS, HEAD_DV, nb * GLA_DK), F32),
            pltpu.VMEM((HG_HEADS, HEAD_DV, nb * HG_DK), F32),
            pltpu.VMEM((S5_BLOCKS, rows, S5_BLOCK_STATE), F32),
            pltpu.VMEM((S5_BLOCKS, nb, S5_BLOCK_STATE), F32),
            pltpu.VMEM((nb, BRANCH_W), F32),
            pltpu.VMEM(((CONV_W - 1) * nb, BRANCH_W), F32),
            pltpu.VMEM((d_model // LANES, rows, LANES), F32),
            pltpu.VMEM((2, rows, d_model), F32),
            pltpu.VMEM((2, rows, d_model), BF16),
        ],
        compiler_params=pltpu.CompilerParams(dimension_semantics=("arbitrary",),
                                             vmem_limit_bytes=VMEM_LIMIT_BYTES),
        name="hybrid_layer_final" if final else "hybrid_layer",
    )(h, head, *weights)


def _block_diag(blocks):
    n, r, c = blocks.shape
    eye = jnp.eye(n, dtype=blocks.dtype)
    return (blocks[:, :, None, :] * eye[:, None, :, None]).reshape(n * r, n * c)


def _layer_params(l, lb, norm_g, w_in, w_branch, w_out, gla_w_lr, gla_b_lr, gla_norm, hg_norm,
                  s5_lambda_re, s5_lambda_im, s5_log_dt, s5_b_re, s5_b_im, s5_c_re, s5_c_im, s5_d,
                  s5_glu_w, s5_glu_b, lru_conv_w, lru_conv_b, lru_wa, lru_ba, lru_wx, lru_bx, lru_lambda):
    w = w_in[l]
    d_model = w.shape[0]
    sizes = (GLA_HEADS * GLA_DK, GLA_HEADS * GLA_DK, GLA_HEADS * HEAD_DV, 16, BRANCH_W,
             HG_HEADS * HG_DK, HG_HEADS * HG_DK, HG_HEADS * HEAD_DV, BRANCH_W,
             BRANCH_W, BRANCH_W, BRANCH_W, BRANCH_W, 4 * d_model)
    offs = [0]
    for s in sizes:
        offs.append(offs[-1] + s)
    col = lambda a, b: w[:, offs[a]:offs[b]].astype(BF16)
    rank = sizes[3]
    w_gla_lr = jnp.pad(w[:, offs[3]:offs[4]], ((0, 0), (0, LANES - rank))).astype(BF16)
    gla_w_lr_p = jnp.pad(gla_w_lr[l], ((0, LANES - rank), (0, 0))).astype(BF16)

    lr_, li_ = s5_lambda_re[l].astype(F32), s5_lambda_im[l].astype(F32)
    dt = jnp.exp(s5_log_dt[l].astype(F32))[:, None]
    mag = jnp.exp(lr_ * dt)
    ab_re, ab_im = mag * jnp.cos(li_ * dt), mag * jnp.sin(li_ * dt)
    nr, ni = ab_re - 1.0, ab_im
    den = lr_ * lr_ + li_ * li_
    cr, ci = (nr * lr_ + ni * li_) / den, (ni * lr_ - nr * li_) / den
    br, bi = s5_b_re[l].astype(F32), s5_b_im[l].astype(F32)
    bb_re = cr[..., None] * br - ci[..., None] * bi
    bb_im = cr[..., None] * bi + ci[..., None] * br
    gpb = LANES // S5_GROUP

    def in_block(bb):
        t = jnp.swapaxes(bb, 1, 2).reshape(S5_BLOCKS, gpb, S5_GROUP, S5_STATE)
        return jnp.stack([_block_diag(t[j]) for j in range(S5_BLOCKS)])

    def out_block(c):
        t = c.astype(F32).reshape(S5_BLOCKS, gpb, S5_GROUP, S5_STATE)
        return jnp.stack([_block_diag(t[j]) for j in range(S5_BLOCKS)])

    s5_bw = jnp.concatenate([in_block(bb_re), in_block(bb_im)], axis=2).astype(BF16)
    s5_cw = jnp.concatenate([out_block(s5_c_re[l]), -out_block(s5_c_im[l])], axis=2).astype(BF16)
    s5_are = ab_re.reshape(S5_BLOCKS, gpb * S5_STATE)
    s5_aim = ab_im.reshape(S5_BLOCKS, gpb * S5_STATE)

    lru_wax = jnp.concatenate([_block_diag(lru_wa[l]), _block_diag(lru_wx[l])], axis=1).astype(BF16)
    lru_bax = jnp.concatenate([lru_ba[l], lru_bx[l]])[None, :]
    row = lambda a: a[None, :].astype(F32)
    return [
        row(norm_g[l]),
        jnp.concatenate([col(0, 1), col(1, 2), col(2, 3), w_gla_lr], axis=1), col(4, 5),
        gla_w_lr_p, row(gla_b_lr[l]), gla_norm[l].astype(F32).reshape(1, -1),
        jnp.concatenate([col(5, 6), col(6, 7), col(7, 8), col(8, 9)], axis=1), row(lb),
        hg_norm[l].astype(F32).reshape(1, -1),
        jnp.concatenate([col(9, 10), col(10, 11)], axis=1), s5_bw, s5_cw, s5_are, s5_aim, row(s5_d[l]),
        s5_glu_w[l].astype(BF16), row(s5_glu_b[l]),
        jnp.concatenate([col(11, 12), col(12, 13)], axis=1), lru_conv_w[l].astype(F32), row(lru_conv_b[l]),
        lru_wax, lru_bax.astype(F32), row(lru_lambda[l]),
        col(13, 14), w_branch[l].astype(BF16), w_out[l].astype(BF16),
    ]


def kernel(x, meta_tokens, hgrn_lb_logits, final_norm, norm_g, w_in, w_branch, w_out, gla_w_lr, gla_b_lr, gla_norm, hg_norm, s5_lambda_re, s5_lambda_im, s5_log_dt, s5_b_re, s5_b_im, s5_c_re, s5_c_im, s5_d, s5_glu_w, s5_glu_b, lru_conv_w, lru_conv_b, lru_wa, lru_ba, lru_wx, lru_bx, lru_lambda):
    nb, seq, d_model = x.shape
    depth = w_in.shape[0]
    assert seq % STEPS == 0 and N_META <= STEPS and depth >= 2
    meta = jnp.broadcast_to(meta_tokens.astype(F32)[:, None, :], (N_META, nb, d_model))
    head = jnp.concatenate([jnp.zeros((STEPS - N_META, nb, d_model), F32), meta], axis=0)
    head = head.reshape(STEPS * nb, d_model)

    p = jax.nn.softmax(hgrn_lb_logits.astype(F32), axis=0)
    lb_all = jnp.cumsum(p, axis=0) - p[0:1]
    fin = final_norm[None, :].astype(F32)
    h = x.astype(F32)
    for l in range(depth):
        params = _layer_params(l, lb_all[l], norm_g, w_in, w_branch, w_out, gla_w_lr, gla_b_lr, gla_norm,
                               hg_norm, s5_lambda_re, s5_lambda_im, s5_log_dt, s5_b_re, s5_b_im, s5_c_re,
                               s5_c_im, s5_d, s5_glu_w, s5_glu_b, lru_conv_w, lru_conv_b, lru_wa, lru_ba,
                               lru_wx, lru_bx, lru_lambda)
        h = _layer_call(h, head, params, fin, nb=nb, seq=seq, first=(l == 0), final=(l == depth - 1))
    return h
```

```python
import functools
import math

import jax
import jax.numpy as jnp
from jax import lax
from jax.experimental import pallas as pl
from jax.experimental.pallas import tpu as pltpu

F32 = jnp.float32
BF16 = jnp.bfloat16

N_META = 16
EPS = 1e-6
STEPS = 32
LANES = 128
HEAD_DV = 128
BRANCH_W = 512
GLA_HEADS, GLA_DK, GLA_TAU = 4, 64, 16.0
HG_HEADS, HG_DK = 4, 128
S5_GROUP, S5_STATE = 16, 64
S5_BLOCKS = 4
S5_BLOCK_STATE = 2 * 8 * S5_STATE
N_BRANCH = 4
LRU_BLOCKS, LRU_C, CONV_W = 8, 8.0, 4
VMEM_LIMIT_BYTES = 58 * 1024 * 1024

GLA_RANK = 16
D_MODEL = 1024
_W_QK = GLA_HEADS * GLA_DK
IN_QKV_LR = 2 * _W_QK + GLA_HEADS * HEAD_DV + LANES
IN_GATE_A = IN_QKV_LR + BRANCH_W
IN_HG = IN_GATE_A + 3 * HG_HEADS * HG_DK + BRANCH_W
IN_S5 = IN_HG + 2 * BRANCH_W
IN_LRU = IN_S5 + 2 * BRANCH_W
IN_TOTAL = IN_LRU + N_BRANCH * D_MODEL
(V_NORM_G, V_FINAL_NORM, V_GLA_B_LR, V_GLA_NORM, V_HG_LB, V_HG_NORM, V_S5_D, V_GLU_B, V_CONV_B,
 V_LRU_BAX, V_LRU_LAM) = range(11)
V_CONV_W = 11
V_S5_ARE = V_CONV_W + CONV_W
V_S5_AIM = V_S5_ARE + S5_BLOCKS
V_ROWS = -(-(V_S5_AIM + S5_BLOCKS) // 8) * 8


def _sigmoid(x):
    return 0.5 * jnp.tanh(0.5 * x) + 0.5


def _softplus(x):
    return jnp.maximum(x, 0.0) + jnp.log1p(jnp.exp(-jnp.abs(x)))


def _sigmoid_exp(x):
    return 1.0 / (1.0 + jnp.exp(-x))


def _silu(x):
    return x * _sigmoid(x)


def _gelu_tanh(x):
    c = math.sqrt(2.0 / math.pi)
    return 0.5 * x * (1.0 + jnp.tanh(c * (x + 0.044715 * (x * x * x))))


def _rms_norm(x, gain):
    return x * lax.rsqrt(jnp.mean(x * x, axis=-1, keepdims=True) + EPS) * gain


def _mm(a, w):
    return jnp.dot(a.astype(BF16), w, preferred_element_type=F32)


def _mm_nt(a, b):
    return lax.dot_general(a.astype(BF16), b.astype(BF16), (((1,), (1,)), ((), ())),
                           preferred_element_type=F32)


def _head_dup(arr, h, dk):
    p = (h * dk) // LANES
    blk = arr[:, p * LANES:(p + 1) * LANES]
    if dk == LANES:
        return blk
    assert dk * 2 == LANES
    lane = lax.broadcasted_iota(jnp.int32, blk.shape, 1)
    in_head = (lane < dk) if (h % 2 == 0) else (lane >= dk)
    return jnp.where(in_head, blk, pltpu.roll(blk, dk, axis=1))


def _chunk_attention(q, k, v, g, st_ref, norm_g, dk, heads, nb, between_stages):
    rows, width = q.shape
    steps = rows // nb
    g3 = g.reshape(steps, nb, width)
    acc = g3[0]
    cum = [acc]
    for t in range(1, steps):
        acc = acc + g3[t]
        cum.append(acc)
    g_cum = jnp.concatenate(cum, axis=0)
    g_last = acc
    g_last_rows = jnp.concatenate([g_last] * steps, axis=0)
    q_dec = q * jnp.exp(g_cum)
    k_inv = k * jnp.exp(-g_cum)
    k_end = k * jnp.exp(g_last_rows - g_cum)

    exp_w = nb * dk
    reps = exp_w // LANES
    r_i = lax.broadcasted_iota(jnp.int32, (rows, rows), 0)
    c_i = lax.broadcasted_iota(jnp.int32, (rows, rows), 1)
    causal_same_batch_t = ((r_i % nb) == (c_i % nb)) & ((r_i // nb) <= (c_i // nb))
    batch_mask = (lax.broadcasted_iota(jnp.int32, (rows, exp_w), 0) % nb
                  == lax.broadcasted_iota(jnp.int32, (rows, exp_w), 1) // dk)
    batch_mask_t = (lax.broadcasted_iota(jnp.int32, (exp_w, rows), 0) // dk
                    == lax.broadcasted_iota(jnp.int32, (exp_w, rows), 1) % nb)
    batch_mask_nb = (lax.broadcasted_iota(jnp.int32, (nb, exp_w), 0)
                     == lax.broadcasted_iota(jnp.int32, (nb, exp_w), 1) // dk)

    v_t = v.T.astype(BF16)
    q_t = q_dec.T
    hr = range(heads)
    v_th = [v_t[h * HEAD_DV:(h + 1) * HEAD_DV, :] for h in hr]
    scores_t = []
    for h in hr:
        p = (h * dk) // LANES
        q_blk = q_t[p * LANES:(p + 1) * LANES, :]
        if dk < LANES:
            row = lax.broadcasted_iota(jnp.int32, q_blk.shape, 0)
            q_blk = jnp.where((row < dk) if (h % 2 == 0) else (row >= dk), q_blk, 0.0)
        scores_t.append(_mm(k_inv[:, p * LANES:(p + 1) * LANES], q_blk.astype(BF16)))
    between_stages()
    q_exp_t = [jnp.where(batch_mask_t, jnp.concatenate([q_t[h * dk:(h + 1) * dk, :]] * nb, axis=0),
                         0.0).astype(BF16) for h in hr]
    scores_t = [jnp.where(causal_same_batch_t, s, 0.0).astype(BF16) for s in scores_t]
    st = [st_ref[h] for h in hr]
    o_t = [_mm(jnp.concatenate([v_th[h], st[h].astype(BF16)], axis=1),
               jnp.concatenate([scores_t[h], q_exp_t[h]], axis=0)) for h in hr]
    between_stages()
    k_exp = [jnp.where(batch_mask, jnp.concatenate([_head_dup(k_end, h, dk)] * reps, axis=1),
                       0.0).astype(BF16) for h in hr]
    for h in hr:
        gl_dup = _head_dup(g_last, h, dk)
        gl_flat = jnp.sum(jnp.where(batch_mask_nb, jnp.concatenate([gl_dup] * reps, axis=1), 0.0),
                          axis=0, keepdims=True)
        st_ref[h] = st[h] * jnp.exp(gl_flat) + _mm(v_th[h], k_exp[h])
    between_stages()
    outs_t = [o * lax.rsqrt(jnp.mean(o * o, axis=0, keepdims=True) + EPS) for o in o_t]
    out = jnp.concatenate(outs_t, axis=0).T * norm_g
    between_stages()
    return out


def _layer_kernel(next_ref, head_ref, w_in_ref, vec_ref, gla_w_lr_ref, s5_bw_ref, s5_cw_ref,
                  s5_glu_w_ref, lru_wax_ref, w_branch_ref, w_out_ref,
                  out_ref,
                  gla_st_ref, hg_st_ref, s5_scan_ref, s5_st_ref, lru_st_ref, conv_st_ref, relayout_ref,
                  h_buf_ref, xn_buf_ref,
                  *, nb, first, final):
    step_idx = pl.program_id(0)
    cur = step_idx % 2

    def vec(row, width):
        return vec_ref[row:row + 1, :width]

    norm_g = vec(V_NORM_G, D_MODEL)

    @pl.when(step_idx == 0)
    def _init():
        gla_st_ref[...] = jnp.zeros_like(gla_st_ref)
        hg_st_ref[...] = jnp.zeros_like(hg_st_ref)
        s5_st_ref[...] = jnp.zeros_like(s5_st_ref)
        lru_st_ref[...] = jnp.zeros_like(lru_st_ref)
        conv_st_ref[...] = jnp.zeros_like(conv_st_ref)
        h_buf_ref[0] = head_ref[...]
        xn_buf_ref[0] = _rms_norm(head_ref[...], norm_g).astype(BF16)

    rows, d_model = head_ref.shape
    steps = rows // nb
    lane_slabs = d_model // LANES
    xn = xn_buf_ref[cur]

    def prepare_next_block():
        if first:
            for b in range(nb):
                for c in range(lane_slabs):
                    relayout_ref[c, pl.ds(b, steps, stride=nb), :] = next_ref[b, :, c * LANES:(c + 1) * LANES]
            h_next = jnp.concatenate([relayout_ref[c] for c in range(lane_slabs)], axis=1)
        else:
            h_next = next_ref[...]
        h_buf_ref[1 - cur] = h_next
        xn_buf_ref[1 - cur] = _rms_norm(h_next, norm_g).astype(BF16)

    gate_cols = 2 * LANES
    gate_chunks = []

    def emit_gate_chunks(n):
        for _ in range(n):
            c = len(gate_chunks)
            if c * gate_cols < N_BRANCH * d_model:
                gate_chunks.append(_sigmoid(_mm(xn, w_in_ref[:, IN_LRU + c * gate_cols:IN_LRU + (c + 1) * gate_cols])))

    def branch_gate(n):
        per = d_model // gate_cols
        emit_gate_chunks((n + 1) * per - len(gate_chunks))
        return jnp.concatenate(gate_chunks[n * per:(n + 1) * per], axis=1)

    qkv = _mm(xn, w_in_ref[:, :IN_QKV_LR])
    w_q = GLA_HEADS * GLA_DK
    w_qkv = 2 * w_q + GLA_HEADS * HEAD_DV
    logit = _mm(qkv[:, w_qkv:], gla_w_lr_ref[...]) + vec(V_GLA_B_LR, _W_QK)
    hz = _mm(xn, w_in_ref[:, IN_GATE_A:IN_HG])
    sz = _mm(xn, w_in_ref[:, IN_HG:IN_S5])
    lz = _mm(xn, w_in_ref[:, IN_S5:IN_LRU])
    gla_gate = _mm(xn, w_in_ref[:, IN_QKV_LR:IN_GATE_A])
    prepare_next_block()
    log_a = -_softplus(-logit) * (1.0 / GLA_TAU)

    lx = lz[:, :BRANCH_W]
    hist = (CONV_W - 1) * nb
    x_full = jnp.concatenate([conv_st_ref[...], lx], axis=0)
    conv_st_ref[...] = lx[rows - hist:, :]
    xc = vec(V_CONV_B, BRANCH_W)
    for j in range(CONV_W):
        xc = xc + x_full[hist - j * nb: hist - j * nb + rows, :] * vec(V_CONV_W + j, BRANCH_W)
    ax = _mm(xc, lru_wax_ref[...]) + vec(V_LRU_BAX, 2 * BRANCH_W)
    r_g = _sigmoid(ax[:, :BRANCH_W])
    i_g = _sigmoid(ax[:, BRANCH_W:])
    log_al = (-LRU_C) * r_g * _softplus(-vec(V_LRU_LAM, BRANCH_W))
    a_l = jnp.exp(log_al)
    inp = jnp.sqrt(1.0 - jnp.exp(2.0 * log_al)) * (i_g * xc)
    pad_rows = (STEPS - N_META) * nb
    row_id = step_idx * rows + lax.broadcasted_iota(jnp.int32, inp.shape, 0)
    inp = jnp.where(row_id >= pad_rows, inp, 0.0)
    s_l = lru_st_ref[...]
    hs = []
    for t in range(steps):
        s_l = a_l[t * nb:(t + 1) * nb, :] * s_l + inp[t * nb:(t + 1) * nb, :]
        hs.append(s_l)
    lru_st_ref[...] = s_l
    y_d = jnp.concatenate(hs, axis=0) * _silu(lz[:, BRANCH_W:])

    y_a = _chunk_attention(qkv[:, :w_q] * (GLA_DK ** -0.5), qkv[:, w_q:2 * w_q], qkv[:, 2 * w_q:w_qkv],
                           log_a, gla_st_ref, vec(V_GLA_NORM, BRANCH_W), GLA_DK, GLA_HEADS, nb,
                           lambda: emit_gate_chunks(1))
    y_a = y_a * _silu(gla_gate)
    merged = branch_gate(0) * _mm(y_a, w_branch_ref[0])

    w_h = HG_HEADS * HG_DK
    lb = vec(V_HG_LB, BRANCH_W)
    f = lb + (1.0 - lb) * _sigmoid_exp(hz[:, w_h:2 * w_h])
    y_b = _chunk_attention(hz[:, :w_h], 1.0 - f, hz[:, 2 * w_h:3 * w_h], jnp.log(f),
                           hg_st_ref, vec(V_HG_NORM, BRANCH_W), HG_DK, HG_HEADS, nb,
                           lambda: emit_gate_chunks(1))
    y_b = y_b * _silu(hz[:, 3 * w_h:])
    merged = merged + branch_gate(1) * _mm(y_b, w_branch_ref[1])

    u = sz[:, :BRANCH_W]
    half = S5_BLOCK_STATE // 2
    for j in range(S5_BLOCKS):
        s5_scan_ref[j] = _mm(u[:, j * LANES:(j + 1) * LANES], s5_bw_ref[j])
    y_blocks = []
    for j in range(S5_BLOCKS):
        a_re = jnp.broadcast_to(vec(V_S5_ARE + j, half), (nb, half))
        a_im = jnp.broadcast_to(vec(V_S5_AIM + j, half), (nb, half))
        s_re = s5_st_ref[j, :, :half]
        s_im = s5_st_ref[j, :, half:]
        emit_gate_chunks(1)
        for t in range(steps):
            b_re = s5_scan_ref[j, t * nb:(t + 1) * nb, :half]
            b_im = s5_scan_ref[j, t * nb:(t + 1) * nb, half:]
            s_re, s_im = (a_re * s_re - a_im * s_im + b_re, a_re * s_im + a_im * s_re + b_im)
            s5_scan_ref[j, t * nb:(t + 1) * nb, :half] = s_re
            s5_scan_ref[j, t * nb:(t + 1) * nb, half:] = s_im
        s5_st_ref[j, :, :half] = s_re
        s5_st_ref[j, :, half:] = s_im
        y_blocks.append(_mm_nt(s5_cw_ref[j], s5_scan_ref[j]))
    y_c = jnp.concatenate(y_blocks, axis=0).T + vec(V_S5_D, BRANCH_W) * u
    y_c = _gelu_tanh(y_c)
    y_c = y_c * _sigmoid(_mm(y_c, s5_glu_w_ref[...]) + vec(V_GLU_B, BRANCH_W))
    y_c = y_c * _silu(sz[:, BRANCH_W:])
    merged = merged + branch_gate(2) * _mm(y_c, w_branch_ref[2])

    merged = merged + branch_gate(3) * _mm(y_d, w_branch_ref[3])

    out = h_buf_ref[cur] + _mm(merged, w_out_ref[...])
    if final:
        out = _rms_norm(out, vec(V_FINAL_NORM, D_MODEL))
        for c in range(lane_slabs):
            relayout_ref[c] = out[:, c * LANES:(c + 1) * LANES]
        for b in range(nb):
            for c in range(lane_slabs):
                out_ref[b, :, c * LANES:(c + 1) * LANES] = relayout_ref[c, pl.ds(b, steps, stride=nb), :]
    else:
        out_ref[...] = out


def _resident(arr):
    nd = arr.ndim
    return pl.BlockSpec(arr.shape, lambda i, _nd=nd: (0,) * _nd, pipeline_mode=pl.Buffered(1))


def _resident_layer(arr, layer):
    nd = arr.ndim - 1
    return pl.BlockSpec((None,) + arr.shape[1:], lambda i, _nd=nd: (layer,) + (0,) * _nd,
                        pipeline_mode=pl.Buffered(1))


def _layer_call(h, head, params, layer, *, nb, seq, first, final):
    rows, d_model = head.shape
    n_blocks = 1 + seq // STEPS
    kern = functools.partial(_layer_kernel, nb=nb, first=first, final=final)
    if first:
        next_spec = pl.BlockSpec((nb, STEPS, d_model), lambda i: (0, jnp.minimum(i, n_blocks - 2), 0))
        head_spec = _resident(head)
    else:
        next_spec = pl.BlockSpec((rows, d_model), lambda i: (jnp.minimum(i + 1, n_blocks - 1), 0))
        head_spec = pl.BlockSpec((rows, d_model), lambda i: (0, 0), pipeline_mode=pl.Buffered(1))
        head = h
    out_batch_major = pl.BlockSpec((nb, STEPS, d_model), lambda i: (0, jnp.maximum(i - 1, 0), 0))
    return pl.pallas_call(
        kern,
        grid=(n_blocks,),
        in_specs=[next_spec, head_spec] + [_resident_layer(w, layer) for w in params],
        out_specs=out_batch_major if final else pl.BlockSpec((rows, d_model), lambda i: (i, 0)),
        out_shape=jax.ShapeDtypeStruct((nb, seq, d_model) if final else (n_blocks * rows, d_model), F32),
        scratch_shapes=[
            pltpu.VMEM((GLA_HEADS, HEAD_DV, nb * GLA_DK), F32),
            pltpu.VMEM((HG_HEADS, HEAD_DV, nb * HG_DK), F32),
            pltpu.VMEM((S5_BLOCKS, rows, S5_BLOCK_STATE), F32),
            pltpu.VMEM((S5_BLOCKS, nb, S5_BLOCK_STATE), F32),
            pltpu.VMEM((nb, BRANCH_W), F32),
            pltpu.VMEM(((CONV_W - 1) * nb, BRANCH_W), F32),
            pltpu.VMEM((d_model // LANES, rows, LANES), F32),
            pltpu.VMEM((2, rows, d_model), F32),
            pltpu.VMEM((2, rows, d_model), BF16),
        ],
        compiler_params=pltpu.CompilerParams(dimension_semantics=("arbitrary",),
                                             vmem_limit_bytes=VMEM_LIMIT_BYTES),
        name="hybrid_layer_final" if final else "hybrid_layer",
    )(h, head, *params)


def _block_diag(blocks):
    n, r, c = blocks.shape[-3:]
    eye = jnp.eye(n, dtype=blocks.dtype)
    out = blocks[..., :, :, None, :] * eye[:, None, :, None]
    return out.reshape(blocks.shape[:-3] + (n * r, n * c))


def _pack_params(lb_all, final_norm, norm_g, w_in, w_branch, w_out, gla_w_lr, gla_b_lr, gla_norm, hg_norm,
                 s5_lambda_re, s5_lambda_im, s5_log_dt, s5_b_re, s5_b_im, s5_c_re, s5_c_im, s5_d,
                 s5_glu_w, s5_glu_b, lru_conv_w, lru_conv_b, lru_wa, lru_ba, lru_wx, lru_bx, lru_lambda):
    depth, d_model, _ = w_in.shape
    assert d_model == D_MODEL and gla_w_lr.shape[1] == GLA_RANK
    lr_end = 2 * _W_QK + GLA_HEADS * HEAD_DV + GLA_RANK
    w_all = jnp.concatenate([w_in[..., :lr_end], jnp.zeros((depth, d_model, LANES - GLA_RANK), w_in.dtype),
                             w_in[..., lr_end:]], axis=-1).astype(BF16)
    assert w_all.shape[-1] == IN_TOTAL
    gla_w_lr_p = jnp.pad(gla_w_lr, ((0, 0), (0, LANES - GLA_RANK), (0, 0))).astype(BF16)

    lr_, li_ = s5_lambda_re.astype(F32), s5_lambda_im.astype(F32)
    dt = jnp.exp(s5_log_dt.astype(F32))[..., None]
    mag = jnp.exp(lr_ * dt)
    ab_re, ab_im = mag * jnp.cos(li_ * dt), mag * jnp.sin(li_ * dt)
    nr, ni = ab_re - 1.0, ab_im
    den = lr_ * lr_ + li_ * li_
    cr, ci = (nr * lr_ + ni * li_) / den, (ni * lr_ - nr * li_) / den
    br, bi = s5_b_re.astype(F32), s5_b_im.astype(F32)
    bb_re = cr[..., None] * br - ci[..., None] * bi
    bb_im = cr[..., None] * bi + ci[..., None] * br
    gpb = LANES // S5_GROUP

    def in_block(bb):
        return _block_diag(jnp.swapaxes(bb, -1, -2).reshape(depth, S5_BLOCKS, gpb, S5_GROUP, S5_STATE))

    def out_block(c):
        return _block_diag(c.astype(F32).reshape(depth, S5_BLOCKS, gpb, S5_GROUP, S5_STATE))

    s5_bw = jnp.concatenate([in_block(bb_re), in_block(bb_im)], axis=-1).astype(BF16)
    s5_cw = jnp.concatenate([out_block(s5_c_re), -out_block(s5_c_im)], axis=-1).astype(BF16)
    lru_wax = jnp.concatenate([_block_diag(lru_wa), _block_diag(lru_wx)], axis=-1).astype(BF16)

    def rows_of(a):
        a = a.astype(F32).reshape(depth, -1, a.shape[-1])
        return jnp.pad(a, ((0, 0), (0, 0), (0, d_model - a.shape[-1])))

    slab = [None] * V_ROWS
    slab[V_NORM_G] = rows_of(norm_g)
    slab[V_FINAL_NORM] = rows_of(jnp.broadcast_to(final_norm, (depth, d_model)))
    slab[V_GLA_B_LR] = rows_of(gla_b_lr)
    slab[V_GLA_NORM] = rows_of(gla_norm.reshape(depth, -1))
    slab[V_HG_LB] = rows_of(lb_all)
    slab[V_HG_NORM] = rows_of(hg_norm.reshape(depth, -1))
    slab[V_S5_D] = rows_of(s5_d)
    slab[V_GLU_B] = rows_of(s5_glu_b)
    slab[V_CONV_B] = rows_of(lru_conv_b)
    slab[V_LRU_BAX] = rows_of(jnp.concatenate([lru_ba, lru_bx], axis=-1))
    slab[V_LRU_LAM] = rows_of(lru_lambda)
    slab[V_CONV_W] = rows_of(lru_conv_w)
    slab[V_S5_ARE] = rows_of(ab_re.reshape(depth, S5_BLOCKS, gpb * S5_STATE))
    slab[V_S5_AIM] = rows_of(ab_im.reshape(depth, S5_BLOCKS, gpb * S5_STATE))
    used = V_S5_AIM + S5_BLOCKS
    parts = [p for p in slab if p is not None] + [jnp.zeros((depth, V_ROWS - used, d_model), F32)]
    vec = jnp.concatenate(parts, axis=1)
    assert vec.shape == (depth, V_ROWS, d_model)
    return [w_all, vec, gla_w_lr_p, s5_bw, s5_cw, s5_glu_w.astype(BF16), lru_wax,
            w_branch.astype(BF16), w_out.astype(BF16)]


def kernel(x, meta_tokens, hgrn_lb_logits, final_norm, norm_g, w_in, w_branch, w_out, gla_w_lr, gla_b_lr, gla_norm, hg_norm, s5_lambda_re, s5_lambda_im, s5_log_dt, s5_b_re, s5_b_im, s5_c_re, s5_c_im, s5_d, s5_glu_w, s5_glu_b, lru_conv_w, lru_conv_b, lru_wa, lru_ba, lru_wx, lru_bx, lru_lambda):
    nb, seq, d_model = x.shape
    depth = w_in.shape[0]
    assert seq % STEPS == 0 and N_META <= STEPS and depth >= 2
    meta = jnp.broadcast_to(meta_tokens.astype(F32)[:, None, :], (N_META, nb, d_model))
    head = jnp.concatenate([jnp.zeros((STEPS - N_META, nb, d_model), F32), meta], axis=0)
    head = head.reshape(STEPS * nb, d_model)

    p = jax.nn.softmax(hgrn_lb_logits.astype(F32), axis=0)
    lb_all = jnp.cumsum(p, axis=0) - p[0:1]
    params = _pack_params(lb_all, final_norm, norm_g, w_in, w_branch, w_out, gla_w_lr, gla_b_lr, gla_norm,
                          hg_norm, s5_lambda_re, s5_lambda_im, s5_log_dt, s5_b_re, s5_b_im, s5_c_re, s5_c_im,
                          s5_d, s5_glu_w, s5_glu_b, lru_conv_w, lru_conv_b, lru_wa, lru_ba, lru_wx, lru_bx,
                          lru_lambda)
    h = x.astype(F32)
    for l in range(depth):
        h = _layer_call(h, head, params, l, nb=nb, seq=seq, first=(l == 0), final=(l == depth - 1))
    return h
```

```python
import functools
import math

import jax
import jax.numpy as jnp
from jax import lax
from jax.experimental import pallas as pl
from jax.experimental.pallas import tpu as pltpu

F32 = jnp.float32
BF16 = jnp.bfloat16

N_META = 16
EPS = 1e-6
STEPS = 32
CHUNK = 32
LANES = 128
HEAD_DV = 128
BRANCH_W = 512
GLA_HEADS, GLA_DK, GLA_TAU = 4, 64, 16.0
HG_HEADS, HG_DK = 4, 128
S5_GROUP, S5_STATE = 16, 64
S5_BLOCKS = 4
S5_BLOCK_STATE = 2 * 8 * S5_STATE
N_BRANCH = 4
LRU_BLOCKS, LRU_C, CONV_W = 8, 8.0, 4
VMEM_LIMIT_BYTES = 58 * 1024 * 1024

GLA_RANK = 16
D_MODEL = 1024
_W_QK = GLA_HEADS * GLA_DK
IN_QKV = 2 * _W_QK + GLA_HEADS * HEAD_DV
IN_GATE_A = BRANCH_W
IN_HG = IN_GATE_A + 3 * HG_HEADS * HG_DK + BRANCH_W
IN_S5 = IN_HG + 2 * BRANCH_W
IN_LRU = IN_S5 + 2 * BRANCH_W
IN_REST = IN_LRU + N_BRANCH * D_MODEL
(V_NORM_G, V_FINAL_NORM, V_GLA_B_LR, V_GLA_NORM, V_HG_LB, V_HG_NORM, V_S5_D, V_GLU_B, V_CONV_B,
 V_LRU_BAX, V_LRU_LAM) = range(11)
V_CONV_W = 11
V_S5_ARE = V_CONV_W + CONV_W
V_S5_AIM = V_S5_ARE + S5_BLOCKS
V_ROWS = -(-(V_S5_AIM + S5_BLOCKS) // 8) * 8


def _sigmoid(x):
    return 0.5 * jnp.tanh(0.5 * x) + 0.5


def _softplus(x):
    return jnp.maximum(x, 0.0) + jnp.log1p(jnp.exp(-jnp.abs(x)))


def _sigmoid_exp(x):
    return 1.0 / (1.0 + jnp.exp(-x))


def _silu(x):
    return x * _sigmoid(x)


def _gelu_tanh(x):
    c = math.sqrt(2.0 / math.pi)
    return 0.5 * x * (1.0 + jnp.tanh(c * (x + 0.044715 * (x * x * x))))


def _rms_norm(x, gain):
    return x * lax.rsqrt(jnp.mean(x * x, axis=-1, keepdims=True) + EPS) * gain


def _mm(a, w):
    return jnp.dot(a.astype(BF16), w, preferred_element_type=F32)


def _mm_nt(a, b):
    return lax.dot_general(a.astype(BF16), b.astype(BF16), (((1,), (1,)), ((), ())),
                           preferred_element_type=F32)


def _head_dup(arr, h, dk):
    p = (h * dk) // LANES
    blk = arr[:, p * LANES:(p + 1) * LANES]
    if dk == LANES:
        return blk
    assert dk * 2 == LANES
    lane = lax.broadcasted_iota(jnp.int32, blk.shape, 1)
    in_head = (lane < dk) if (h % 2 == 0) else (lane >= dk)
    return jnp.where(in_head, blk, pltpu.roll(blk, dk, axis=1))


def _chunk_attention(q, k, v, g, st_ref, norm_g, dk, heads, nb):
    rows, width = q.shape
    steps = rows // nb
    g3 = g.reshape(steps, nb, width)
    acc = g3[0]
    cum = [acc]
    for t in range(1, steps):
        acc = acc + g3[t]
        cum.append(acc)
    g_cum = jnp.concatenate(cum, axis=0)
    g_last = acc
    g_last_rows = jnp.concatenate([g_last] * steps, axis=0)
    q_dec = q * jnp.exp(g_cum)
    k_inv = k * jnp.exp(-g_cum)
    k_end = k * jnp.exp(g_last_rows - g_cum)

    exp_w = nb * dk
    reps = exp_w // LANES
    r_i = lax.broadcasted_iota(jnp.int32, (rows, rows), 0)
    c_i = lax.broadcasted_iota(jnp.int32, (rows, rows), 1)
    causal_same_batch_t = ((r_i % nb) == (c_i % nb)) & ((r_i // nb) <= (c_i // nb))
    batch_mask = (lax.broadcasted_iota(jnp.int32, (rows, exp_w), 0) % nb
                  == lax.broadcasted_iota(jnp.int32, (rows, exp_w), 1) // dk)
    batch_mask_t = (lax.broadcasted_iota(jnp.int32, (exp_w, rows), 0) // dk
                    == lax.broadcasted_iota(jnp.int32, (exp_w, rows), 1) % nb)
    batch_mask_nb = (lax.broadcasted_iota(jnp.int32, (nb, exp_w), 0)
                     == lax.broadcasted_iota(jnp.int32, (nb, exp_w), 1) // dk)

    v_t = v.T.astype(BF16)
    q_t = q_dec.T
    hr = range(heads)
    v_th = [v_t[h * HEAD_DV:(h + 1) * HEAD_DV, :] for h in hr]
    scores_t = []
    for h in hr:
        p = (h * dk) // LANES
        q_blk = q_t[p * LANES:(p + 1) * LANES, :]
        if dk < LANES:
            row = lax.broadcasted_iota(jnp.int32, q_blk.shape, 0)
            q_blk = jnp.where((row < dk) if (h % 2 == 0) else (row >= dk), q_blk, 0.0)
        scores_t.append(_mm(k_inv[:, p * LANES:(p + 1) * LANES], q_blk.astype(BF16)))
    yield
    q_exp_t = [jnp.where(batch_mask_t, jnp.concatenate([q_t[h * dk:(h + 1) * dk, :]] * nb, axis=0),
                         0.0).astype(BF16) for h in hr]
    scores_t = [jnp.where(causal_same_batch_t, s, 0.0).astype(BF16) for s in scores_t]
    st = [st_ref[h] for h in hr]
    o_t = [_mm(jnp.concatenate([v_th[h], st[h].astype(BF16)], axis=1),
               jnp.concatenate([scores_t[h], q_exp_t[h]], axis=0)) for h in hr]
    yield
    k_exp = [jnp.where(batch_mask, jnp.concatenate([_head_dup(k_end, h, dk)] * reps, axis=1),
                       0.0).astype(BF16) for h in hr]
    for h in hr:
        gl_dup = _head_dup(g_last, h, dk)
        gl_flat = jnp.sum(jnp.where(batch_mask_nb, jnp.concatenate([gl_dup] * reps, axis=1), 0.0),
                          axis=0, keepdims=True)
        st_ref[h] = st[h] * jnp.exp(gl_flat) + _mm(v_th[h], k_exp[h])
    yield
    outs_t = [o * lax.rsqrt(jnp.mean(o * o, axis=0, keepdims=True) + EPS) for o in o_t]
    out = jnp.concatenate(outs_t, axis=0).T * norm_g
    yield
    return out


def _attention_block(q, k, v, g, st_ref, norm_g, dk, heads, nb):
    chunk_rows = CHUNK * nb
    outs = []
    for s in range(0, q.shape[0], chunk_rows):
        out = yield from _chunk_attention(q[s:s + chunk_rows], k[s:s + chunk_rows], v[s:s + chunk_rows],
                                          g[s:s + chunk_rows], st_ref, norm_g, dk, heads, nb)
        outs.append(out)
    return jnp.concatenate(outs, axis=0)


def _run_interleaved(stage_generators, between_rounds):
    results = [None] * len(stage_generators)
    active = list(enumerate(stage_generators))
    while active:
        still_active = []
        for i, gen in active:
            try:
                next(gen)
                still_active.append((i, gen))
            except StopIteration as done:
                results[i] = done.value
        active = still_active
        if active:
            between_rounds()
    return results


def _layer_kernel(next_ref, head_ref, w_qkvl_ref, w_rest_ref, vec_ref, gla_w_lr_ref, s5_bw_ref, s5_cw_ref,
                  s5_glu_w_ref, lru_wax_ref, w_branch_ref, w_out_ref,
                  out_ref,
                  gla_st_ref, hg_st_ref, s5_scan_ref, s5_st_ref, lru_st_ref, conv_st_ref, relayout_ref,
                  h_buf_ref, xn_buf_ref,
                  *, nb, first, final):
    step_idx = pl.program_id(0)
    cur = step_idx % 2

    def vec(row, width):
        return vec_ref[row:row + 1, :width]

    norm_g = vec(V_NORM_G, D_MODEL)

    @pl.when(step_idx == 0)
    def _init():
        gla_st_ref[...] = jnp.zeros_like(gla_st_ref)
        hg_st_ref[...] = jnp.zeros_like(hg_st_ref)
        s5_st_ref[...] = jnp.zeros_like(s5_st_ref)
        lru_st_ref[...] = jnp.zeros_like(lru_st_ref)
        conv_st_ref[...] = jnp.zeros_like(conv_st_ref)
        h_buf_ref[0] = head_ref[...]
        xn_buf_ref[0] = _rms_norm(head_ref[...], norm_g).astype(BF16)

    rows, d_model = head_ref.shape
    steps = rows // nb
    lane_slabs = d_model // LANES
    xn = xn_buf_ref[cur]

    def prepare_next_block():
        if first:
            for b in range(nb):
                for c in range(lane_slabs):
                    relayout_ref[c, pl.ds(b, steps, stride=nb), :] = next_ref[b, :, c * LANES:(c + 1) * LANES]
            h_next = jnp.concatenate([relayout_ref[c] for c in range(lane_slabs)], axis=1)
        else:
            h_next = next_ref[...]
        h_buf_ref[1 - cur] = h_next
        xn_buf_ref[1 - cur] = _rms_norm(h_next, norm_g).astype(BF16)

    gate_cols = 2 * LANES
    gate_chunks = []

    def emit_gate_chunks(n):
        for _ in range(n):
            c = len(gate_chunks)
            if c * gate_cols < N_BRANCH * d_model:
                gate_chunks.append(_sigmoid(_mm(xn, w_rest_ref[:, IN_LRU + c * gate_cols:IN_LRU + (c + 1) * gate_cols])))

    def branch_gate(n):
        per = d_model // gate_cols
        emit_gate_chunks((n + 1) * per - len(gate_chunks))
        return jnp.concatenate(gate_chunks[n * per:(n + 1) * per], axis=1)

    qkv = _mm(xn, w_qkvl_ref[...])
    w_q = GLA_HEADS * GLA_DK
    logit = _mm(qkv[:, IN_QKV:], gla_w_lr_ref[...]) + vec(V_GLA_B_LR, _W_QK)
    hz = _mm(xn, w_rest_ref[:, IN_GATE_A:IN_HG])
    sz = _mm(xn, w_rest_ref[:, IN_HG:IN_S5])
    lz = _mm(xn, w_rest_ref[:, IN_S5:IN_LRU])
    gla_gate = _mm(xn, w_rest_ref[:, :IN_GATE_A])
    log_a = -_softplus(-logit) * (1.0 / GLA_TAU)

    def lru_mixer():
        lx = lz[:, :BRANCH_W]
        hist = (CONV_W - 1) * nb
        x_full = jnp.concatenate([conv_st_ref[...], lx], axis=0)
        conv_st_ref[...] = lx[rows - hist:, :]
        xc = vec(V_CONV_B, BRANCH_W)
        for j in range(CONV_W):
            xc = xc + x_full[hist - j * nb: hist - j * nb + rows, :] * vec(V_CONV_W + j, BRANCH_W)
        ax = _mm(xc, lru_wax_ref[...]) + vec(V_LRU_BAX, 2 * BRANCH_W)
        r_g = _sigmoid(ax[:, :BRANCH_W])
        i_g = _sigmoid(ax[:, BRANCH_W:])
        log_al = (-LRU_C) * r_g * _softplus(-vec(V_LRU_LAM, BRANCH_W))
        a_l = jnp.exp(log_al)
        inp = jnp.sqrt(1.0 - jnp.exp(2.0 * log_al)) * (i_g * xc)
        pad_rows = (STEPS - N_META) * nb
        row_id = step_idx * rows + lax.broadcasted_iota(jnp.int32, inp.shape, 0)
        inp = jnp.where(row_id >= pad_rows, inp, 0.0)
        s_l = lru_st_ref[...]
        hs = []
        for t in range(steps):
            s_l = a_l[t * nb:(t + 1) * nb, :] * s_l + inp[t * nb:(t + 1) * nb, :]
            hs.append(s_l)
        lru_st_ref[...] = s_l
        return jnp.concatenate(hs, axis=0) * _silu(lz[:, BRANCH_W:])

    prepare_next_block()
    y_d = lru_mixer()

    (y_a,) = _run_interleaved(
        [_attention_block(qkv[:, :w_q] * (GLA_DK ** -0.5), qkv[:, w_q:2 * w_q], qkv[:, 2 * w_q:IN_QKV],
                          log_a, gla_st_ref, vec(V_GLA_NORM, BRANCH_W), GLA_DK, GLA_HEADS, nb)],
        lambda: emit_gate_chunks(1))
    y_a = y_a * _silu(gla_gate)
    merged = branch_gate(0) * _mm(y_a, w_branch_ref[0])

    w_h = HG_HEADS * HG_DK
    lb = vec(V_HG_LB, BRANCH_W)
    f = lb + (1.0 - lb) * _sigmoid_exp(hz[:, w_h:2 * w_h])
    (y_b,) = _run_interleaved(
        [_attention_block(hz[:, :w_h], 1.0 - f, hz[:, 2 * w_h:3 * w_h], jnp.log(f),
                          hg_st_ref, vec(V_HG_NORM, BRANCH_W), HG_DK, HG_HEADS, nb)],
        lambda: emit_gate_chunks(1))
    y_b = y_b * _silu(hz[:, 3 * w_h:])
    merged = merged + branch_gate(1) * _mm(y_b, w_branch_ref[1])

    u = sz[:, :BRANCH_W]
    half = S5_BLOCK_STATE // 2
    for j in range(S5_BLOCKS):
        s5_scan_ref[j] = _mm(u[:, j * LANES:(j + 1) * LANES], s5_bw_ref[j])
    y_blocks = []
    for j in range(S5_BLOCKS):
        a_re = jnp.broadcast_to(vec(V_S5_ARE + j, half), (nb, half))
        a_im = jnp.broadcast_to(vec(V_S5_AIM + j, half), (nb, half))
        s_re = s5_st_ref[j, :, :half]
        s_im = s5_st_ref[j, :, half:]
        emit_gate_chunks(1)
        for t in range(steps):
            b_re = s5_scan_ref[j, t * nb:(t + 1) * nb, :half]
            b_im = s5_scan_ref[j, t * nb:(t + 1) * nb, half:]
            s_re, s_im = (a_re * s_re - a_im * s_im + b_re, a_re * s_im + a_im * s_re + b_im)
            s5_scan_ref[j, t * nb:(t + 1) * nb, :half] = s_re
            s5_scan_ref[j, t * nb:(t + 1) * nb, half:] = s_im
        s5_st_ref[j, :, :half] = s_re
        s5_st_ref[j, :, half:] = s_im
        y_blocks.append(_mm_nt(s5_cw_ref[j], s5_scan_ref[j]))
    y_c = jnp.concatenate(y_blocks, axis=0).T + vec(V_S5_D, BRANCH_W) * u
    y_c = _gelu_tanh(y_c)
    y_c = y_c * _sigmoid(_mm(y_c, s5_glu_w_ref[...]) + vec(V_GLU_B, BRANCH_W))
    y_c = y_c * _silu(sz[:, BRANCH_W:])
    merged = merged + branch_gate(2) * _mm(y_c, w_branch_ref[2])

    merged = merged + branch_gate(3) * _mm(y_d, w_branch_ref[3])

    out = h_buf_ref[cur] + _mm(merged, w_out_ref[...])
    if final:
        out = _rms_norm(out, vec(V_FINAL_NORM, D_MODEL))
        for c in range(lane_slabs):
            relayout_ref[c] = out[:, c * LANES:(c + 1) * LANES]
        for b in range(nb):
            for c in range(lane_slabs):
                out_ref[b, :, c * LANES:(c + 1) * LANES] = relayout_ref[c, pl.ds(b, steps, stride=nb), :]
    else:
        out_ref[...] = out


def _resident(arr):
    nd = arr.ndim
    return pl.BlockSpec(arr.shape, lambda i, _nd=nd: (0,) * _nd, pipeline_mode=pl.Buffered(1))


def _resident_layer(arr, layer):
    nd = arr.ndim - 1
    return pl.BlockSpec((None,) + arr.shape[1:], lambda i, _nd=nd: (layer,) + (0,) * _nd,
                        pipeline_mode=pl.Buffered(1))


def _layer_call(h, head, params, layer, *, nb, seq, first, final):
    rows, d_model = head.shape
    n_blocks = 1 + seq // STEPS
    kern = functools.partial(_layer_kernel, nb=nb, first=first, final=final)
    if first:
        next_spec = pl.BlockSpec((nb, STEPS, d_model), lambda i: (0, jnp.minimum(i, n_blocks - 2), 0))
        head_spec = _resident(head)
    else:
        next_spec = pl.BlockSpec((rows, d_model), lambda i: (jnp.minimum(i + 1, n_blocks - 1), 0))
        head_spec = pl.BlockSpec((rows, d_model), lambda i: (0, 0), pipeline_mode=pl.Buffered(1))
        head = h
    out_batch_major = pl.BlockSpec((nb, STEPS, d_model), lambda i: (0, jnp.maximum(i - 1, 0), 0))
    return pl.pallas_call(
        kern,
        grid=(n_blocks,),
        in_specs=[next_spec, head_spec] + [_resident_layer(w, layer) for w in params],
        out_specs=out_batch_major if final else pl.BlockSpec((rows, d_model), lambda i: (i, 0)),
        out_shape=jax.ShapeDtypeStruct((nb, seq, d_model) if final else (n_blocks * rows, d_model), F32),
        scratch_shapes=[
            pltpu.VMEM((GLA_HEADS, HEAD_DV, nb * GLA_DK), F32),
            pltpu.VMEM((HG_HEADS, HEAD_DV, nb * HG_DK), F32),
            pltpu.VMEM((S5_BLOCKS, rows, S5_BLOCK_STATE), F32),
            pltpu.VMEM((S5_BLOCKS, nb, S5_BLOCK_STATE), F32),
            pltpu.VMEM((nb, BRANCH_W), F32),
            pltpu.VMEM(((CONV_W - 1) * nb, BRANCH_W), F32),
            pltpu.VMEM((d_model // LANES, rows, LANES), F32),
            pltpu.VMEM((2, rows, d_model), F32),
            pltpu.VMEM((2, rows, d_model), BF16),
        ],
        compiler_params=pltpu.CompilerParams(dimension_semantics=("arbitrary",),
                                             vmem_limit_bytes=VMEM_LIMIT_BYTES),
        name="hybrid_layer_final" if final else "hybrid_layer",
    )(h, head, *params)


def _block_diag(blocks):
    n, r, c = blocks.shape[-3:]
    eye = jnp.eye(n, dtype=blocks.dtype)
    out = blocks[..., :, :, None, :] * eye[:, None, :, None]
    return out.reshape(blocks.shape[:-3] + (n * r, n * c))


def _pack_params(lb_all, final_norm, norm_g, w_in, w_branch, w_out, gla_w_lr, gla_b_lr, gla_norm, hg_norm,
                 s5_lambda_re, s5_lambda_im, s5_log_dt, s5_b_re, s5_b_im, s5_c_re, s5_c_im, s5_d,
                 s5_glu_w, s5_glu_b, lru_conv_w, lru_conv_b, lru_wa, lru_ba, lru_wx, lru_bx, lru_lambda):
    depth, d_model, _ = w_in.shape
    assert d_model == D_MODEL and gla_w_lr.shape[1] == GLA_RANK
    w_qkvl = jnp.pad(w_in[..., :IN_QKV + GLA_RANK], ((0, 0), (0, 0), (0, LANES - GLA_RANK))).astype(BF16)
    w_rest = w_in[..., IN_QKV + GLA_RANK:].astype(BF16)
    assert w_rest.shape[-1] == IN_REST
    gla_w_lr_p = jnp.pad(gla_w_lr, ((0, 0), (0, LANES - GLA_RANK), (0, 0))).astype(BF16)

    lr_, li_ = s5_lambda_re.astype(F32), s5_lambda_im.astype(F32)
    dt = jnp.exp(s5_log_dt.astype(F32))[..., None]
    mag = jnp.exp(lr_ * dt)
    ab_re, ab_im = mag * jnp.cos(li_ * dt), mag * jnp.sin(li_ * dt)
    nr, ni = ab_re - 1.0, ab_im
    den = lr_ * lr_ + li_ * li_
    cr, ci = (nr * lr_ + ni * li_) / den, (ni * lr_ - nr * li_) / den
    br, bi = s5_b_re.astype(F32), s5_b_im.astype(F32)
    bb_re = cr[..., None] * br - ci[..., None] * bi
    bb_im = cr[..., None] * bi + ci[..., None] * br
    gpb = LANES // S5_GROUP

    def in_block(bb):
        return _block_diag(jnp.swapaxes(bb, -1, -2).reshape(depth, S5_BLOCKS, gpb, S5_GROUP, S5_STATE))

    def out_block(c):
        return _block_diag(c.astype(F32).reshape(depth, S5_BLOCKS, gpb, S5_GROUP, S5_STATE))

    s5_bw = jnp.concatenate([in_block(bb_re), in_block(bb_im)], axis=-1).astype(BF16)
    s5_cw = jnp.concatenate([out_block(s5_c_re), -out_block(s5_c_im)], axis=-1).astype(BF16)
    lru_wax = jnp.concatenate([_block_diag(lru_wa), _block_diag(lru_wx)], axis=-1).astype(BF16)

    def rows_of(a):
        a = a.astype(F32).reshape(depth, -1, a.shape[-1])
        return jnp.pad(a, ((0, 0), (0, 0), (0, d_model - a.shape[-1])))

    slab = [None] * V_ROWS
    slab[V_NORM_G] = rows_of(norm_g)
    slab[V_FINAL_NORM] = rows_of(jnp.broadcast_to(final_norm, (depth, d_model)))
    slab[V_GLA_B_LR] = rows_of(gla_b_lr)
    slab[V_GLA_NORM] = rows_of(gla_norm.reshape(depth, -1))
    slab[V_HG_LB] = rows_of(lb_all)
    slab[V_HG_NORM] = rows_of(hg_norm.reshape(depth, -1))
    slab[V_S5_D] = rows_of(s5_d)
    slab[V_GLU_B] = rows_of(s5_glu_b)
    slab[V_CONV_B] = rows_of(lru_conv_b)
    slab[V_LRU_BAX] = rows_of(jnp.concatenate([lru_ba, lru_bx], axis=-1))
    slab[V_LRU_LAM] = rows_of(lru_lambda)
    slab[V_CONV_W] = rows_of(lru_conv_w)
    slab[V_S5_ARE] = rows_of(ab_re.reshape(depth, S5_BLOCKS, gpb * S5_STATE))
    slab[V_S5_AIM] = rows_of(ab_im.reshape(depth, S5_BLOCKS, gpb * S5_STATE))
    used = V_S5_AIM + S5_BLOCKS
    parts = [p for p in slab if p is not None] + [jnp.zeros((depth, V_ROWS - used, d_model), F32)]
    vec = jnp.concatenate(parts, axis=1)
    assert vec.shape == (depth, V_ROWS, d_model)
    return [w_qkvl, w_rest, vec, gla_w_lr_p, s5_bw, s5_cw, s5_glu_w.astype(BF16), lru_wax,
            w_branch.astype(BF16), w_out.astype(BF16)]


def kernel(x, meta_tokens, hgrn_lb_logits, final_norm, norm_g, w_in, w_branch, w_out, gla_w_lr, gla_b_lr, gla_norm, hg_norm, s5_lambda_re, s5_lambda_im, s5_log_dt, s5_b_re, s5_b_im, s5_c_re, s5_c_im, s5_d, s5_glu_w, s5_glu_b, lru_conv_w, lru_conv_b, lru_wa, lru_ba, lru_wx, lru_bx, lru_lambda):
    nb, seq, d_model = x.shape
    depth = w_in.shape[0]
    assert seq % STEPS == 0 and N_META <= STEPS and depth >= 2
    meta = jnp.broadcast_to(meta_tokens.astype(F32)[:, None, :], (N_META, nb, d_model))
    head = jnp.concatenate([jnp.zeros((STEPS - N_META, nb, d_model), F32), meta], axis=0)
    head = head.reshape(STEPS * nb, d_model)

    p = jax.nn.softmax(hgrn_lb_logits.astype(F32), axis=0)
    lb_all = jnp.cumsum(p, axis=0) - p[0:1]
    params = _pack_params(lb_all, final_norm, norm_g, w_in, w_branch, w_out, gla_w_lr, gla_b_lr, gla_norm,
                          hg_norm, s5_lambda_re, s5_lambda_im, s5_log_dt, s5_b_re, s5_b_im, s5_c_re, s5_c_im,
                          s5_d, s5_glu_w, s5_glu_b, lru_conv_w, lru_conv_b, lru_wa, lru_ba, lru_wx, lru_bx,
                          lru_lambda)
    h = x.astype(F32)
    for l in range(depth):
        h = _layer_call(h, head, params, l, nb=nb, seq=seq, first=(l == 0), final=(l == depth - 1))
    return h
```

```python
import functools
import math

import jax
import jax.numpy as jnp
from jax import lax
from jax.experimental import pallas as pl
from jax.experimental.pallas import tpu as pltpu

F32 = jnp.float32
BF16 = jnp.bfloat16

N_META = 16
EPS = 1e-6
STEPS = 32
CHUNK = 32
LANES = 128
HEAD_DV = 128
BRANCH_W = 512
GLA_HEADS, GLA_DK, GLA_TAU = 4, 64, 16.0
HG_HEADS, HG_DK = 4, 128
S5_GROUP, S5_STATE = 16, 64
S5_BLOCKS = 4
S5_BLOCK_STATE = 2 * 8 * S5_STATE
N_BRANCH = 4
LRU_BLOCKS, LRU_C, CONV_W = 8, 8.0, 4
VMEM_LIMIT_BYTES = 58 * 1024 * 1024

GLA_RANK = 16
D_MODEL = 1024
_W_QK = GLA_HEADS * GLA_DK
IN_QKV = 2 * _W_QK + GLA_HEADS * HEAD_DV
IN_GATE_A = BRANCH_W
IN_HG = IN_GATE_A + 3 * HG_HEADS * HG_DK + BRANCH_W
IN_S5 = IN_HG + 2 * BRANCH_W
IN_LRU = IN_S5 + 2 * BRANCH_W
IN_REST = IN_LRU + N_BRANCH * D_MODEL
(V_NORM_G, V_FINAL_NORM, V_GLA_B_LR, V_GLA_NORM, V_HG_LB, V_HG_NORM, V_S5_D, V_GLU_B, V_CONV_B,
 V_LRU_BAX, V_LRU_LAM) = range(11)
V_CONV_W = 11
V_S5_ARE = V_CONV_W + CONV_W
V_S5_AIM = V_S5_ARE + S5_BLOCKS
V_ROWS = -(-(V_S5_AIM + S5_BLOCKS) // 8) * 8


def _sigmoid(x):
    return 0.5 * jnp.tanh(0.5 * x) + 0.5


def _softplus(x):
    return jnp.maximum(x, 0.0) + jnp.log1p(jnp.exp(-jnp.abs(x)))


def _sigmoid_exp(x):
    return 1.0 / (1.0 + jnp.exp(-x))


def _silu(x):
    return x * _sigmoid(x)


def _gelu_tanh(x):
    c = math.sqrt(2.0 / math.pi)
    return 0.5 * x * (1.0 + jnp.tanh(c * (x + 0.044715 * (x * x * x))))


def _rms_norm(x, gain):
    return x * lax.rsqrt(jnp.mean(x * x, axis=-1, keepdims=True) + EPS) * gain


def _mm(a, w):
    return jnp.dot(a.astype(BF16), w, preferred_element_type=F32)


def _mm_nt(a, b):
    return lax.dot_general(a.astype(BF16), b.astype(BF16), (((1,), (1,)), ((), ())),
                           preferred_element_type=F32)


def _head_dup(arr, h, dk):
    p = (h * dk) // LANES
    blk = arr[:, p * LANES:(p + 1) * LANES]
    if dk == LANES:
        return blk
    assert dk * 2 == LANES
    lane = lax.broadcasted_iota(jnp.int32, blk.shape, 1)
    in_head = (lane < dk) if (h % 2 == 0) else (lane >= dk)
    return jnp.where(in_head, blk, pltpu.roll(blk, dk, axis=1))


def _chunk_attention(q, k, v, g, st_ref, norm_g, dk, heads, nb):
    rows, width = q.shape
    steps = rows // nb
    g3 = g.reshape(steps, nb, width)
    acc = g3[0]
    cum = [acc]
    for t in range(1, steps):
        acc = acc + g3[t]
        cum.append(acc)
    g_cum = jnp.concatenate(cum, axis=0)
    g_last = acc
    g_last_rows = jnp.concatenate([g_last] * steps, axis=0)
    q_dec = q * jnp.exp(g_cum)
    k_inv = k * jnp.exp(-g_cum)
    k_end = k * jnp.exp(g_last_rows - g_cum)

    exp_w = nb * dk
    reps = exp_w // LANES
    r_i = lax.broadcasted_iota(jnp.int32, (rows, rows), 0)
    c_i = lax.broadcasted_iota(jnp.int32, (rows, rows), 1)
    causal_same_batch_t = ((r_i % nb) == (c_i % nb)) & ((r_i // nb) <= (c_i // nb))
    batch_mask = (lax.broadcasted_iota(jnp.int32, (rows, exp_w), 0) % nb
                  == lax.broadcasted_iota(jnp.int32, (rows, exp_w), 1) // dk)
    batch_mask_t = (lax.broadcasted_iota(jnp.int32, (exp_w, rows), 0) // dk
                    == lax.broadcasted_iota(jnp.int32, (exp_w, rows), 1) % nb)
    batch_mask_nb = (lax.broadcasted_iota(jnp.int32, (nb, exp_w), 0)
                     == lax.broadcasted_iota(jnp.int32, (nb, exp_w), 1) // dk)

    v_t = v.T.astype(BF16)
    q_t = q_dec.T
    hr = range(heads)
    v_th = [v_t[h * HEAD_DV:(h + 1) * HEAD_DV, :] for h in hr]
    scores_t = []
    for h in hr:
        p = (h * dk) // LANES
        q_blk = q_t[p * LANES:(p + 1) * LANES, :]
        if dk < LANES:
            row = lax.broadcasted_iota(jnp.int32, q_blk.shape, 0)
            q_blk = jnp.where((row < dk) if (h % 2 == 0) else (row >= dk), q_blk, 0.0)
        scores_t.append(_mm(k_inv[:, p * LANES:(p + 1) * LANES], q_blk.astype(BF16)))
    yield
    q_exp_t = [jnp.where(batch_mask_t, jnp.concatenate([q_t[h * dk:(h + 1) * dk, :]] * nb, axis=0),
                         0.0).astype(BF16) for h in hr]
    scores_t = [jnp.where(causal_same_batch_t, s, 0.0).astype(BF16) for s in scores_t]
    st = [st_ref[h] for h in hr]
    o_t = [_mm(jnp.concatenate([v_th[h], st[h].astype(BF16)], axis=1),
               jnp.concatenate([scores_t[h], q_exp_t[h]], axis=0)) for h in hr]
    yield
    k_exp = [jnp.where(batch_mask, jnp.concatenate([_head_dup(k_end, h, dk)] * reps, axis=1),
                       0.0).astype(BF16) for h in hr]
    for h in hr:
        gl_dup = _head_dup(g_last, h, dk)
        gl_flat = jnp.sum(jnp.where(batch_mask_nb, jnp.concatenate([gl_dup] * reps, axis=1), 0.0),
                          axis=0, keepdims=True)
        st_ref[h] = st[h] * jnp.exp(gl_flat) + _mm(v_th[h], k_exp[h])
    yield
    outs_t = [o * lax.rsqrt(jnp.mean(o * o, axis=0, keepdims=True) + EPS) for o in o_t]
    out = jnp.concatenate(outs_t, axis=0).T * norm_g
    yield
    return out


def _attention_block(q, k, v, g, st_ref, norm_g, dk, heads, nb):
    chunk_rows = CHUNK * nb
    outs = []
    for s in range(0, q.shape[0], chunk_rows):
        out = yield from _chunk_attention(q[s:s + chunk_rows], k[s:s + chunk_rows], v[s:s + chunk_rows],
                                          g[s:s + chunk_rows], st_ref, norm_g, dk, heads, nb)
        outs.append(out)
    return jnp.concatenate(outs, axis=0)


def _run_interleaved(stage_generators, between_rounds):
    results = [None] * len(stage_generators)
    active = list(enumerate(stage_generators))
    while active:
        still_active = []
        for i, gen in active:
            try:
                next(gen)
                still_active.append((i, gen))
            except StopIteration as done:
                results[i] = done.value
        active = still_active
        if active:
            between_rounds()
    return results


def _layer_kernel(next_ref, head_ref, w_qkvl_ref, w_rest_ref, vec_ref, gla_w_lr_ref, s5_bw_ref, s5_cw_ref,
                  s5_glu_w_ref, lru_wax_ref, w_branch_ref, w_out_ref,
                  out_ref,
                  gla_st_ref, hg_st_ref, s5_scan_ref, s5_st_ref, lru_st_ref, conv_st_ref, relayout_ref,
                  h_buf_ref, xn_buf_ref,
                  *, nb, first, final):
    step_idx = pl.program_id(0)
    cur = step_idx % 2

    def vec(row, width):
        return vec_ref[row:row + 1, :width]

    norm_g = vec(V_NORM_G, D_MODEL)

    @pl.when(step_idx == 0)
    def _init():
        gla_st_ref[...] = jnp.zeros_like(gla_st_ref)
        hg_st_ref[...] = jnp.zeros_like(hg_st_ref)
        s5_st_ref[...] = jnp.zeros_like(s5_st_ref)
        lru_st_ref[...] = jnp.zeros_like(lru_st_ref)
        conv_st_ref[...] = jnp.zeros_like(conv_st_ref)
        h_buf_ref[0] = head_ref[...]
        xn_buf_ref[0] = _rms_norm(head_ref[...], norm_g).astype(BF16)

    rows, d_model = head_ref.shape
    steps = rows // nb
    lane_slabs = d_model // LANES
    xn = xn_buf_ref[cur]

    def prepare_next_block():
        if first:
            for b in range(nb):
                for c in range(lane_slabs):
                    relayout_ref[c, pl.ds(b, steps, stride=nb), :] = next_ref[b, :, c * LANES:(c + 1) * LANES]
            h_next = jnp.concatenate([relayout_ref[c] for c in range(lane_slabs)], axis=1)
        else:
            h_next = next_ref[...]
        h_buf_ref[1 - cur] = h_next
        xn_buf_ref[1 - cur] = _rms_norm(h_next, norm_g).astype(BF16)

    gate_cols = 2 * LANES
    gate_chunks = []

    def emit_gate_chunks(n):
        for _ in range(n):
            c = len(gate_chunks)
            if c * gate_cols < N_BRANCH * d_model:
                gate_chunks.append(_sigmoid(_mm(xn, w_rest_ref[:, IN_LRU + c * gate_cols:IN_LRU + (c + 1) * gate_cols])))

    def branch_gate(n):
        per = d_model // gate_cols
        emit_gate_chunks((n + 1) * per - len(gate_chunks))
        return jnp.concatenate(gate_chunks[n * per:(n + 1) * per], axis=1)

    qkv = _mm(xn, w_qkvl_ref[...])
    w_q = GLA_HEADS * GLA_DK
    logit = _mm(qkv[:, IN_QKV:], gla_w_lr_ref[...]) + vec(V_GLA_B_LR, _W_QK)
    hz = _mm(xn, w_rest_ref[:, IN_GATE_A:IN_HG])
    sz = _mm(xn, w_rest_ref[:, IN_HG:IN_S5])
    lz = _mm(xn, w_rest_ref[:, IN_S5:IN_LRU])
    gla_gate = _mm(xn, w_rest_ref[:, :IN_GATE_A])
    log_a = -_softplus(-logit) * (1.0 / GLA_TAU)

    def lru_mixer():
        lx = lz[:, :BRANCH_W]
        hist = (CONV_W - 1) * nb
        x_full = jnp.concatenate([conv_st_ref[...], lx], axis=0)
        conv_st_ref[...] = lx[rows - hist:, :]
        xc = vec(V_CONV_B, BRANCH_W)
        for j in range(CONV_W):
            xc = xc + x_full[hist - j * nb: hist - j * nb + rows, :] * vec(V_CONV_W + j, BRANCH_W)
        ax = [_mm(xc[:, p * LANES:(p + 1) * LANES], lru_wax_ref[p]) for p in range(BRANCH_W // LANES)]
        bax = vec(V_LRU_BAX, 2 * BRANCH_W)
        r_g = _sigmoid(jnp.concatenate([a[:, :LANES] for a in ax], axis=1) + bax[:, :BRANCH_W])
        i_g = _sigmoid(jnp.concatenate([a[:, LANES:] for a in ax], axis=1) + bax[:, BRANCH_W:])
        log_al = (-LRU_C) * r_g * _softplus(-vec(V_LRU_LAM, BRANCH_W))
        a_l = jnp.exp(log_al)
        inp = jnp.sqrt(1.0 - jnp.exp(2.0 * log_al)) * (i_g * xc)
        pad_rows = (STEPS - N_META) * nb
        row_id = step_idx * rows + lax.broadcasted_iota(jnp.int32, inp.shape, 0)
        inp = jnp.where(row_id >= pad_rows, inp, 0.0)
        s_l = lru_st_ref[...]
        hs = []
        for t in range(steps):
            s_l = a_l[t * nb:(t + 1) * nb, :] * s_l + inp[t * nb:(t + 1) * nb, :]
            hs.append(s_l)
        lru_st_ref[...] = s_l
        return jnp.concatenate(hs, axis=0) * _silu(lz[:, BRANCH_W:])

    prepare_next_block()

    (y_a,) = _run_interleaved(
        [_attention_block(qkv[:, :w_q] * (GLA_DK ** -0.5), qkv[:, w_q:2 * w_q], qkv[:, 2 * w_q:IN_QKV],
                          log_a, gla_st_ref, vec(V_GLA_NORM, BRANCH_W), GLA_DK, GLA_HEADS, nb)],
        lambda: emit_gate_chunks(1))
    y_a = y_a * _silu(gla_gate)
    merged = branch_gate(0) * _mm(y_a, w_branch_ref[0])

    y_d = lru_mixer()

    w_h = HG_HEADS * HG_DK
    lb = vec(V_HG_LB, BRANCH_W)
    f = lb + (1.0 - lb) * _sigmoid_exp(hz[:, w_h:2 * w_h])
    (y_b,) = _run_interleaved(
        [_attention_block(hz[:, :w_h], 1.0 - f, hz[:, 2 * w_h:3 * w_h], jnp.log(f),
                          hg_st_ref, vec(V_HG_NORM, BRANCH_W), HG_DK, HG_HEADS, nb)],
        lambda: emit_gate_chunks(1))
    y_b = y_b * _silu(hz[:, 3 * w_h:])
    merged = merged + branch_gate(1) * _mm(y_b, w_branch_ref[1])

    u = sz[:, :BRANCH_W]
    half = S5_BLOCK_STATE // 2
    for j in range(S5_BLOCKS):
        s5_scan_ref[j] = _mm(u[:, j * LANES:(j + 1) * LANES], s5_bw_ref[j])
    y_blocks = []
    for j in range(S5_BLOCKS):
        a_re = jnp.broadcast_to(vec(V_S5_ARE + j, half), (nb, half))
        a_im = jnp.broadcast_to(vec(V_S5_AIM + j, half), (nb, half))
        s_re = s5_st_ref[j, :, :half]
        s_im = s5_st_ref[j, :, half:]
        emit_gate_chunks(1)
        for t in range(steps):
            b_re = s5_scan_ref[j, t * nb:(t + 1) * nb, :half]
            b_im = s5_scan_ref[j, t * nb:(t + 1) * nb, half:]
            s_re, s_im = (a_re * s_re - a_im * s_im + b_re, a_re * s_im + a_im * s_re + b_im)
            s5_scan_ref[j, t * nb:(t + 1) * nb, :half] = s_re
            s5_scan_ref[j, t * nb:(t + 1) * nb, half:] = s_im
        s5_st_ref[j, :, :half] = s_re
        s5_st_ref[j, :, half:] = s_im
        y_blocks.append(_mm_nt(s5_cw_ref[j], s5_scan_ref[j]))
    y_c = jnp.concatenate(y_blocks, axis=0).T + vec(V_S5_D, BRANCH_W) * u
    y_c = _gelu_tanh(y_c)
    y_c = y_c * _sigmoid(_mm(y_c, s5_glu_w_ref[...]) + vec(V_GLU_B, BRANCH_W))
    y_c = y_c * _silu(sz[:, BRANCH_W:])
    merged = merged + branch_gate(2) * _mm(y_c, w_branch_ref[2])

    merged = merged + branch_gate(3) * _mm(y_d, w_branch_ref[3])

    out = h_buf_ref[cur] + _mm(merged, w_out_ref[...])
    if final:
        out = _rms_norm(out, vec(V_FINAL_NORM, D_MODEL))
        for c in range(lane_slabs):
            relayout_ref[c] = out[:, c * LANES:(c + 1) * LANES]
        for b in range(nb):
            for c in range(lane_slabs):
                out_ref[b, :, c * LANES:(c + 1) * LANES] = relayout_ref[c, pl.ds(b, steps, stride=nb), :]
    else:
        out_ref[...] = out


def _resident(arr):
    nd = arr.ndim
    return pl.BlockSpec(arr.shape, lambda i, _nd=nd: (0,) * _nd, pipeline_mode=pl.Buffered(1))


def _resident_layer(arr, layer):
    nd = arr.ndim - 1
    return pl.BlockSpec((None,) + arr.shape[1:], lambda i, _nd=nd: (layer,) + (0,) * _nd,
                        pipeline_mode=pl.Buffered(1))


def _layer_call(h, head, params, layer, *, nb, seq, first, final):
    rows, d_model = head.shape
    n_blocks = 1 + seq // STEPS
    kern = functools.partial(_layer_kernel, nb=nb, first=first, final=final)
    if first:
        next_spec = pl.BlockSpec((nb, STEPS, d_model), lambda i: (0, jnp.minimum(i, n_blocks - 2), 0))
        head_spec = _resident(head)
    else:
        next_spec = pl.BlockSpec((rows, d_model), lambda i: (jnp.minimum(i + 1, n_blocks - 1), 0))
        head_spec = pl.BlockSpec((rows, d_model), lambda i: (0, 0), pipeline_mode=pl.Buffered(1))
        head = h
    out_batch_major = pl.BlockSpec((nb, STEPS, d_model), lambda i: (0, jnp.maximum(i - 1, 0), 0))
    return pl.pallas_call(
        kern,
        grid=(n_blocks,),
        in_specs=[next_spec, head_spec] + [_resident_layer(w, layer) for w in params],
        out_specs=out_batch_major if final else pl.BlockSpec((rows, d_model), lambda i: (i, 0)),
        out_shape=jax.ShapeDtypeStruct((nb, seq, d_model) if final else (n_blocks * rows, d_model), F32),
        scratch_shapes=[
            pltpu.VMEM((GLA_HEADS, HEAD_DV, nb * GLA_DK), F32),
            pltpu.VMEM((HG_HEADS, HEAD_DV, nb * HG_DK), F32),
            pltpu.VMEM((S5_BLOCKS, rows, S5_BLOCK_STATE), F32),
            pltpu.VMEM((S5_BLOCKS, nb, S5_BLOCK_STATE), F32),
            pltpu.VMEM((nb, BRANCH_W), F32),
            pltpu.VMEM(((CONV_W - 1) * nb, BRANCH_W), F32),
            pltpu.VMEM((d_model // LANES, rows, LANES), F32),
            pltpu.VMEM((2, rows, d_model), F32),
            pltpu.VMEM((2, rows, d_model), BF16),
        ],
        compiler_params=pltpu.CompilerParams(dimension_semantics=("arbitrary",),
                                             vmem_limit_bytes=VMEM_LIMIT_BYTES),
        name="hybrid_layer_final" if final else "hybrid_layer",
    )(h, head, *params)


def _block_diag(blocks):
    n, r, c = blocks.shape[-3:]
    eye = jnp.eye(n, dtype=blocks.dtype)
    out = blocks[..., :, :, None, :] * eye[:, None, :, None]
    return out.reshape(blocks.shape[:-3] + (n * r, n * c))


def _pack_params(lb_all, final_norm, norm_g, w_in, w_branch, w_out, gla_w_lr, gla_b_lr, gla_norm, hg_norm,
                 s5_lambda_re, s5_lambda_im, s5_log_dt, s5_b_re, s5_b_im, s5_c_re, s5_c_im, s5_d,
                 s5_glu_w, s5_glu_b, lru_conv_w, lru_conv_b, lru_wa, lru_ba, lru_wx, lru_bx, lru_lambda):
    depth, d_model, _ = w_in.shape
    assert d_model == D_MODEL and gla_w_lr.shape[1] == GLA_RANK
    w_bf = w_in.astype(BF16)
    w_qkvl = jnp.pad(w_bf[..., :IN_QKV + GLA_RANK], ((0, 0), (0, 0), (0, LANES - GLA_RANK)))
    w_rest = w_bf[..., IN_QKV + GLA_RANK:]
    assert w_rest.shape[-1] == IN_REST
    gla_w_lr_p = jnp.pad(gla_w_lr, ((0, 0), (0, LANES - GLA_RANK), (0, 0))).astype(BF16)

    lr_, li_ = s5_lambda_re.astype(F32), s5_lambda_im.astype(F32)
    dt = jnp.exp(s5_log_dt.astype(F32))[..., None]
    mag = jnp.exp(lr_ * dt)
    ab_re, ab_im = mag * jnp.cos(li_ * dt), mag * jnp.sin(li_ * dt)
    nr, ni = ab_re - 1.0, ab_im
    den = lr_ * lr_ + li_ * li_
    cr, ci = (nr * lr_ + ni * li_) / den, (ni * lr_ - nr * li_) / den
    br, bi = s5_b_re.astype(F32), s5_b_im.astype(F32)
    bb_re = cr[..., None] * br - ci[..., None] * bi
    bb_im = cr[..., None] * bi + ci[..., None] * br
    gpb = LANES // S5_GROUP

    def in_block(bb):
        return _block_diag(jnp.swapaxes(bb, -1, -2).reshape(depth, S5_BLOCKS, gpb, S5_GROUP, S5_STATE))

    def out_block(c):
        return _block_diag(c.astype(F32).reshape(depth, S5_BLOCKS, gpb, S5_GROUP, S5_STATE))

    s5_bw = jnp.concatenate([in_block(bb_re), in_block(bb_im)], axis=-1).astype(BF16)
    s5_cw = jnp.concatenate([out_block(s5_c_re), -out_block(s5_c_im)], axis=-1).astype(BF16)
    pairs = BRANCH_W // LANES

    def pair_blocks(wb):
        return _block_diag(wb.reshape(depth, pairs, LRU_BLOCKS // pairs, *wb.shape[-2:]))

    lru_wax = jnp.concatenate([pair_blocks(lru_wa), pair_blocks(lru_wx)], axis=-1).astype(BF16)

    def rows_of(a):
        a = a.astype(F32).reshape(depth, -1, a.shape[-1])
        return jnp.pad(a, ((0, 0), (0, 0), (0, d_model - a.shape[-1])))

    slab = [None] * V_ROWS
    slab[V_NORM_G] = rows_of(norm_g)
    slab[V_FINAL_NORM] = rows_of(jnp.broadcast_to(final_norm, (depth, d_model)))
    slab[V_GLA_B_LR] = rows_of(gla_b_lr)
    slab[V_GLA_NORM] = rows_of(gla_norm.reshape(depth, -1))
    slab[V_HG_LB] = rows_of(lb_all)
    slab[V_HG_NORM] = rows_of(hg_norm.reshape(depth, -1))
    slab[V_S5_D] = rows_of(s5_d)
    slab[V_GLU_B] = rows_of(s5_glu_b)
    slab[V_CONV_B] = rows_of(lru_conv_b)
    slab[V_LRU_BAX] = rows_of(jnp.concatenate([lru_ba, lru_bx], axis=-1))
    slab[V_LRU_LAM] = rows_of(lru_lambda)
    slab[V_CONV_W] = rows_of(lru_conv_w)
    slab[V_S5_ARE] = rows_of(ab_re.reshape(depth, S5_BLOCKS, gpb * S5_STATE))
    slab[V_S5_AIM] = rows_of(ab_im.reshape(depth, S5_BLOCKS, gpb * S5_STATE))
    used = V_S5_AIM + S5_BLOCKS
    parts = [p for p in slab if p is not None] + [jnp.zeros((depth, V_ROWS - used, d_model), F32)]
    vec = jnp.concatenate(parts, axis=1)
    assert vec.shape == (depth, V_ROWS, d_model)
    return [w_qkvl, w_rest, vec, gla_w_lr_p, s5_bw, s5_cw, s5_glu_w.astype(BF16), lru_wax,
            w_branch.astype(BF16), w_out.astype(BF16)]


def kernel(x, meta_tokens, hgrn_lb_logits, final_norm, norm_g, w_in, w_branch, w_out, gla_w_lr, gla_b_lr, gla_norm, hg_norm, s5_lambda_re, s5_lambda_im, s5_log_dt, s5_b_re, s5_b_im, s5_c_re, s5_c_im, s5_d, s5_glu_w, s5_glu_b, lru_conv_w, lru_conv_b, lru_wa, lru_ba, lru_wx, lru_bx, lru_lambda):
    nb, seq, d_model = x.shape
    depth = w_in.shape[0]
    assert seq % STEPS == 0 and N_META <= STEPS and depth >= 2
    meta = jnp.broadcast_to(meta_tokens.astype(F32)[:, None, :], (N_META, nb, d_model))
    head = jnp.concatenate([jnp.zeros((STEPS - N_META, nb, d_model), F32), meta], axis=0)
    head = head.reshape(STEPS * nb, d_model)

    p = jax.nn.softmax(hgrn_lb_logits.astype(F32), axis=0)
    lb_all = jnp.cumsum(p, axis=0) - p[0:1]
    params = _pack_params(lb_all, final_norm, norm_g, w_in, w_branch, w_out, gla_w_lr, gla_b_lr, gla_norm,
                          hg_norm, s5_lambda_re, s5_lambda_im, s5_log_dt, s5_b_re, s5_b_im, s5_c_re, s5_c_im,
                          s5_d, s5_glu_w, s5_glu_b, lru_conv_w, lru_conv_b, lru_wa, lru_ba, lru_wx, lru_bx,
                          lru_lambda)
    h = x.astype(F32)
    for l in range(depth):
        h = _layer_call(h, head, params, l, nb=nb, seq=seq, first=(l == 0), final=(l == depth - 1))
    return h
```
